```python
import math
import jax, jax.numpy as jnp
from jax import lax
import numpy as np

D_MODEL = 2048
BATCH = 4
SEQ = 2048
DEPTH = 4
DEC_BATCH = 8
DEC_SEQ = 4
PAST_LEN = 16384
PAGE_SIZE = 128

N_MIXERS = 2
N_A_LAYERS = (DEPTH + 1) // 2
N_B_LAYERS = DEPTH // 2
D_RNN = D_MODEL
RG_BLOCK = 256
RG_BLOCKS = D_RNN // RG_BLOCK
CONV_W = 4
LRU_C = 8.0
QK_DIM = 128
V_DIM = 2 * QK_DIM
N_HEADS = D_MODEL // V_DIM
QK_W = N_HEADS * 2 * QK_DIM
V_W = N_HEADS * V_DIM
Q_BLOCK = 128
D_FF = 4 * D_MODEL
EPS = 1e-6

kernel_name = "hybrid_rglru_diffattn_decode_step"


def rms_norm(x, g):
    xf = x.astype(jnp.float32)
    y = xf * lax.rsqrt(jnp.mean(xf * xf, axis=-1, keepdims=True) + EPS)
    return (y * g.astype(jnp.float32)).astype(x.dtype)


def lambda_init_of(layer_idx):
    return 0.8 - 0.6 * math.exp(-0.3 * layer_idx)


def causal_conv(u, buf, w, b):
    T = u.shape[1]
    full = jnp.concatenate([buf.astype(u.dtype), u], axis=1)
    out = b + sum(full[:, k:k + T] * w[k] for k in range(CONV_W))
    return out, full[:, -(CONV_W - 1):]


def rg_lru(u, h0, w_a, b_a, w_i, b_i, lam):
    B, T, C = u.shape
    uf = u.astype(jnp.float32)
    ub = uf.reshape(B, T, RG_BLOCKS, RG_BLOCK)
    r = jax.nn.sigmoid(jnp.einsum('btnc,ncd->btnd', ub, w_a.astype(jnp.float32))
                       + b_a.astype(jnp.float32).reshape(RG_BLOCKS, RG_BLOCK)).reshape(B, T, C)
    i = jax.nn.sigmoid(jnp.einsum('btnc,ncd->btnd', ub, w_i.astype(jnp.float32))
                       + b_i.astype(jnp.float32).reshape(RG_BLOCKS, RG_BLOCK)).reshape(B, T, C)
    log_a = -LRU_C * r * jax.nn.softplus(-lam.astype(jnp.float32))
    a = jnp.exp(log_a)
    mult = jnp.sqrt(-jnp.expm1(2.0 * log_a))
    bx = mult * (i * uf)
    bx = bx.at[:, 0].add(a[:, 0] * h0.astype(jnp.float32))

    def combine(left, right):
        return (left[0] * right[0], right[0] * left[1] + right[1])

    _, h = lax.associative_scan(combine, (a, bx), axis=1)
    return h, h[:, -1]


def recurrent_block(h_in, conv_buf, h0, w_in, conv_w, conv_b, w_a, b_a, w_i, b_i, lam, w_out):
    xz = h_in @ w_in
    xb, gb = xz[..., :D_RNN], xz[..., D_RNN:]
    gate = jax.nn.gelu(gb)
    u, new_buf = causal_conv(xb, conv_buf, conv_w, conv_b)
    h, h_last = rg_lru(u, h0, w_a, b_a, w_i, b_i, lam)
    y = (h.astype(h_in.dtype) * gate) @ w_out
    return y, new_buf, h_last


def da_project(h, w_qkv):
    B, T, _ = h.shape
    qkv = h @ w_qkv
    q = qkv[..., :QK_W].reshape(B, T, N_HEADS, 2, QK_DIM)
    k = qkv[..., QK_W:2 * QK_W].reshape(B, T, N_HEADS, 2, QK_DIM)
    v = qkv[..., 2 * QK_W:].reshape(B, T, N_HEADS, V_DIM)
    return q, k, v


def diff_lambda(lq1, lk1, lq2, lk2, lam_init):
    f = jnp.float32
    return (jnp.exp(jnp.sum(lq1.astype(f) * lk1.astype(f)))
            - jnp.exp(jnp.sum(lq2.astype(f) * lk2.astype(f))) + lam_init)


def diff_attend(q, k, v, mask, lam):
    s = jnp.einsum('bqhmd,bkhmd->bhmqk', q, k,
                   preferred_element_type=jnp.float32) * (QK_DIM ** -0.5)
    s = jnp.where(mask, s, -jnp.inf)
    p = jax.nn.softmax(s, axis=-1)
    attn = p[:, :, 0] - lam * p[:, :, 1]
    return jnp.einsum('bhqk,bkhd->bqhd', attn, v.astype(jnp.float32))


def da_output(o, subln, lam_init, w_o, dtype):
    B, T = o.shape[:2]
    on = o * lax.rsqrt(jnp.mean(o * o, axis=-1, keepdims=True) + EPS)
    on = on * subln.astype(jnp.float32) * (1.0 - lam_init)
    return on.reshape(B, T, V_W).astype(dtype) @ w_o


def diff_attn_prompt(h, w_qkv, lam, lam_init, subln, w_o):
    B, T, _ = h.shape
    q, k, v = da_project(h, w_qkv)
    nqb = T // Q_BLOCK
    qb = jnp.swapaxes(q.reshape(B, nqb, Q_BLOCK, N_HEADS, 2, QK_DIM), 0, 1)
    kpos = jnp.arange(T)

    def one_block(args):
        qi, bi = args
        qpos = bi * Q_BLOCK + jnp.arange(Q_BLOCK)
        return diff_attend(qi, k, v, qpos[:, None] >= kpos[None, :], lam)

    o = lax.map(one_block, (qb, jnp.arange(nqb)))
    o = jnp.swapaxes(o, 0, 1).reshape(B, T, N_HEADS, V_DIM)
    return da_output(o, subln, lam_init, w_o, h.dtype), k, v


def diff_attn_sample(h, cache_k, cache_v, layer, page_table, w_qkv, lam, lam_init, subln, w_o):
    B, T, _ = h.shape
    q, k, v = da_project(h, w_qkv)
    past = page_table.shape[1] * cache_k.shape[2]
    k_past = cache_k[layer, page_table].reshape(B, past, N_HEADS, 2, QK_DIM)
    v_past = cache_v[layer, page_table].reshape(B, past, N_HEADS, V_DIM)
    k_all = jnp.concatenate([k_past.astype(k.dtype), k], axis=1)
    v_all = jnp.concatenate([v_past.astype(v.dtype), v], axis=1)
    mask = jnp.arange(past + T)[None, :] <= (past + jnp.arange(T))[:, None]
    o = diff_attend(q, k_all, v_all, mask, lam)
    return da_output(o, subln, lam_init, w_o, h.dtype), k, v


def sq_relu_mlp(h, w_up, w_down):
    return jnp.square(jax.nn.relu(h @ w_up)) @ w_down


def setup_inputs(seed: int = 0) -> dict:
    key = jax.random.key(seed)
    ks = iter(jax.random.split(key, 40))
    f = jnp.float32
    n_pages = PAST_LEN // PAGE_SIZE
    n_used = DEC_BATCH * n_pages
    n_pool = n_used + max(1, n_used // 4)

    def nrm(shape, scale):
        return jax.random.normal(next(ks), shape, f) * scale

    x_prompt = nrm((BATCH, SEQ, D_MODEL), 1.0)
    x_sample = nrm((DEC_BATCH, DEC_SEQ, D_MODEL), 1.0)
    cache_k = nrm((N_B_LAYERS, n_pool, PAGE_SIZE, N_HEADS, 2, QK_DIM), 1.0)
    cache_v = nrm((N_B_LAYERS, n_pool, PAGE_SIZE, N_HEADS, V_DIM), 1.0)
    state_conv = nrm((N_A_LAYERS, DEC_BATCH, CONV_W - 1, D_RNN), 1.0)
    state_h = nrm((N_A_LAYERS, DEC_BATCH, D_RNN), 0.5)
    perm = jax.random.permutation(next(ks), n_pool)
    page_table = perm[:n_used].reshape(DEC_BATCH, n_pages).astype(jnp.int32)

    norm_mix = 1.0 + nrm((DEPTH, D_MODEL), 0.02)
    norm_mlp = 1.0 + nrm((DEPTH, D_MODEL), 0.02)
    norm_final = 1.0 + nrm((D_MODEL,), 0.02)

    rg_w_in = nrm((N_A_LAYERS, D_MODEL, 2 * D_RNN), D_MODEL ** -0.5)
    rg_conv_w = nrm((N_A_LAYERS, CONV_W, D_RNN), CONV_W ** -0.5)
    rg_conv_b = nrm((N_A_LAYERS, D_RNN), 0.01)
    rg_w_a = nrm((N_A_LAYERS, RG_BLOCKS, RG_BLOCK, RG_BLOCK), RG_BLOCK ** -0.5)
    rg_b_a = nrm((N_A_LAYERS, D_RNN), 0.01)
    rg_w_i = nrm((N_A_LAYERS, RG_BLOCKS, RG_BLOCK, RG_BLOCK), RG_BLOCK ** -0.5)
    rg_b_i = nrm((N_A_LAYERS, D_RNN), 0.01)
    a8 = jax.random.uniform(next(ks), (N_A_LAYERS, D_RNN), f, 0.9, 0.999)
    s = a8 ** (1.0 / LRU_C)
    rg_lambda = jnp.log(s) - jnp.log1p(-s)
    rg_w_out = nrm((N_A_LAYERS, D_RNN, D_MODEL), D_RNN ** -0.5)

    da_w_qkv = nrm((N_B_LAYERS, D_MODEL, 2 * QK_W + V_W), D_MODEL ** -0.5)
    da_lq1 = nrm((N_B_LAYERS, QK_DIM), 0.1)
    da_lk1 = nrm((N_B_LAYERS, QK_DIM), 0.1)
    da_lq2 = nrm((N_B_LAYERS, QK_DIM), 0.1)
    da_lk2 = nrm((N_B_LAYERS, QK_DIM), 0.1)
    da_subln = 1.0 + nrm((N_B_LAYERS, V_DIM), 0.02)
    da_w_o = nrm((N_B_LAYERS, V_W, D_MODEL), V_W ** -0.5)

    mlp_w_up = nrm((DEPTH, D_MODEL, D_FF), D_MODEL ** -0.5)
    mlp_w_down = nrm((DEPTH, D_FF, D_MODEL), D_FF ** -0.5)

    return {"x_prompt": x_prompt, "x_sample": x_sample, "cache_k": cache_k, "cache_v": cache_v,
            "state_conv": state_conv, "state_h": state_h, "page_table": page_table,
            "norm_mix": norm_mix, "norm_mlp": norm_mlp, "norm_final": norm_final,
            "rg_w_in": rg_w_in, "rg_conv_w": rg_conv_w, "rg_conv_b": rg_conv_b,
            "rg_w_a": rg_w_a, "rg_b_a": rg_b_a, "rg_w_i": rg_w_i, "rg_b_i": rg_b_i,
            "rg_lambda": rg_lambda, "rg_w_out": rg_w_out,
            "da_w_qkv": da_w_qkv, "da_lq1": da_lq1, "da_lk1": da_lk1, "da_lq2": da_lq2,
            "da_lk2": da_lk2, "da_subln": da_subln, "da_w_o": da_w_o,
            "mlp_w_up": mlp_w_up, "mlp_w_down": mlp_w_down}


def reference(x_prompt, x_sample, cache_k, cache_v, state_conv, state_h, page_table,
              norm_mix, norm_mlp, norm_final,
              rg_w_in, rg_conv_w, rg_conv_b, rg_w_a, rg_b_a, rg_w_i, rg_b_i, rg_lambda, rg_w_out,
              da_w_qkv, da_lq1, da_lk1, da_lq2, da_lk2, da_subln, da_w_o,
              mlp_w_up, mlp_w_down):
    xp, xs = x_prompt, x_sample
    kp_l, vp_l, cp_l, hp_l = [], [], [], []
    ks_l, vs_l, cs_l, hs_l = [], [], [], []
    for i in range(DEPTH):
        j = i // N_MIXERS
        hp = rms_norm(xp, norm_mix[i])
        hs = rms_norm(xs, norm_mix[i])
        if i % N_MIXERS == 0:
            args = (rg_w_in[j], rg_conv_w[j], rg_conv_b[j], rg_w_a[j], rg_b_a[j],
                    rg_w_i[j], rg_b_i[j], rg_lambda[j], rg_w_out[j])
            zero_buf = jnp.zeros((xp.shape[0], CONV_W - 1, D_RNN), xp.dtype)
            zero_h = jnp.zeros((xp.shape[0], D_RNN), xp.dtype)
            yp, cbp, hlp = recurrent_block(hp, zero_buf, zero_h, *args)
            ys, cbs, hls = recurrent_block(hs, state_conv[j], state_h[j], *args)
            cp_l.append(cbp.astype(state_conv.dtype)); hp_l.append(hlp.astype(state_h.dtype))
            cs_l.append(cbs.astype(state_conv.dtype)); hs_l.append(hls.astype(state_h.dtype))
        else:
            lam_init = lambda_init_of(i)
            lam = diff_lambda(da_lq1[j], da_lk1[j], da_lq2[j], da_lk2[j], lam_init)
            yp, kp, vp = diff_attn_prompt(hp, da_w_qkv[j], lam, lam_init, da_subln[j], da_w_o[j])
            ys, kn, vn = diff_attn_sample(hs, cache_k, cache_v, j, page_table, da_w_qkv[j],
                                          lam, lam_init, da_subln[j], da_w_o[j])
            kp_l.append(kp.astype(cache_k.dtype)); vp_l.append(vp.astype(cache_v.dtype))
            ks_l.append(kn.astype(cache_k.dtype)); vs_l.append(vn.astype(cache_v.dtype))
        xp = xp + yp.astype(xp.dtype)
        xs = xs + ys.astype(xs.dtype)
        xp = xp + sq_relu_mlp(rms_norm(xp, norm_mlp[i]), mlp_w_up[i], mlp_w_down[i]).astype(xp.dtype)
        xs = xs + sq_relu_mlp(rms_norm(xs, norm_mlp[i]), mlp_w_up[i], mlp_w_down[i]).astype(xs.dtype)
    y_prompt = rms_norm(xp, norm_final)
    y_sample = rms_norm(xs, norm_final)
    k_prompt = jnp.stack(kp_l)
    v_prompt = jnp.stack(vp_l)
    conv_prompt = jnp.stack(cp_l)
    h_prompt = jnp.stack(hp_l)
    k_sample = jnp.stack(ks_l)
    v_sample = jnp.stack(vs_l)
    conv_sample = jnp.stack(cs_l)
    h_sample = jnp.stack(hs_l)
    return (y_prompt, y_sample, k_prompt, v_prompt, conv_prompt, h_prompt,
            k_sample, v_sample, conv_sample, h_sample)
```

```python
import functools
import math

import jax
import jax.numpy as jnp
from jax import lax
from jax.experimental import pallas as pl
from jax.experimental.pallas import tpu as pltpu

F32 = jnp.float32
BF16 = jnp.bfloat16

D_MODEL = 2048
DEPTH = 4
PAGE_SIZE = 128
D_RNN = D_MODEL
RG_BLOCK = 256
RG_BLOCKS = D_RNN // RG_BLOCK
CONV_W = 4
LRU_C = 8.0
QK_DIM = 128
V_DIM = 2 * QK_DIM
N_HEADS = D_MODEL // V_DIM
QK_W = N_HEADS * 2 * QK_DIM
V_W = N_HEADS * V_DIM
D_FF = 4 * D_MODEL
EPS = 1e-6
QK_SCALE = QK_DIM ** -0.5

VMEM_LIMIT_BYTES = 56 * 1024 * 1024
SUBLANES = 8

ROW_TILE = 512
COL_TILE = 512
FF_TILE = 512
Q_TILE = 256
SCAN_TILE = 256
PAGES_PER_STEP = 4
CONV_PAD = SUBLANES


def _params(*semantics):
    return pltpu.CompilerParams(dimension_semantics=semantics,
                                vmem_limit_bytes=VMEM_LIMIT_BYTES)


def _rms(x, g):
    return x * lax.rsqrt(jnp.mean(x * x, axis=-1, keepdims=True) + EPS) * g


def _lambda_init(layer_idx):
    return 0.8 - 0.6 * math.exp(-0.3 * layer_idx)


def _norm_proj_kernel(x_ref, g_ref, *refs, n_out, scales):
    w_refs, o_refs, xn_ref = refs[:n_out], refs[n_out:2 * n_out], refs[2 * n_out]

    @pl.when(pl.program_id(1) == 0)
    def _():
        xn_ref[...] = _rms(x_ref[...], g_ref[...]).astype(BF16)

    xn = xn_ref[...]
    for w_ref, o_ref, scale in zip(w_refs, o_refs, scales):
        acc = jnp.dot(xn, w_ref[...], preferred_element_type=F32)
        if scale != 1.0:
            acc = acc * scale
        o_ref[...] = acc.astype(o_ref.dtype)


def _norm_proj(x, g, w, group_width, out_dtypes, scales):
    m, d = x.shape
    n_out = len(out_dtypes)
    assert w.shape == (d, n_out * group_width)
    tm = min(m, ROW_TILE)
    tn = COL_TILE
    nj = group_width // tn
    in_specs = [pl.BlockSpec((tm, d), lambda i, j: (i, 0)),
                pl.BlockSpec((1, d), lambda i, j: (0, 0))]
    for k in range(n_out):
        in_specs.append(pl.BlockSpec((d, tn), lambda i, j, k=k: (0, k * nj + j)))
    return pl.pallas_call(
        functools.partial(_norm_proj_kernel, n_out=n_out, scales=tuple(scales)),
        grid=(m // tm, nj),
        in_specs=in_specs,
        out_specs=[pl.BlockSpec((tm, tn), lambda i, j: (i, j))] * n_out,
        out_shape=[jax.ShapeDtypeStruct((m, group_width), dt) for dt in out_dtypes],
        scratch_shapes=[pltpu.VMEM((tm, d), BF16)],
        compiler_params=_params("parallel", "arbitrary"),
        name="norm_proj",
    )(x, g.reshape(1, d), *([w] * n_out))


def _proj_res_kernel(a_ref, w_ref, r_ref, o_ref):
    o_ref[...] = r_ref[...] + jnp.dot(a_ref[...].astype(BF16), w_ref[...],
                                      preferred_element_type=F32)


def _proj_res(a, w, res):
    m, k = a.shape
    n = w.shape[1]
    tm = min(m, ROW_TILE)
    return pl.pallas_call(
        _proj_res_kernel,
        grid=(m // tm,),
        in_specs=[pl.BlockSpec((tm, k), lambda i: (i, 0)),
                  pl.BlockSpec((k, n), lambda i: (0, 0)),
                  pl.BlockSpec((tm, n), lambda i: (i, 0))],
        out_specs=pl.BlockSpec((tm, n), lambda i: (i, 0)),
        out_shape=jax.ShapeDtypeStruct((m, n), F32),
        compiler_params=_params("parallel"),
        name="proj_res",
    )(a, w, res)


def _mlp_kernel(x_ref, g_ref, wu_ref, wd_ref, gf_ref, o_ref, xn_ref, *, final_norm):
    f = pl.program_id(1)

    @pl.when(f == 0)
    def _():
        x = x_ref[...]
        xn_ref[...] = _rms(x, g_ref[...]).astype(BF16)
        o_ref[...] = x

    h = jnp.dot(xn_ref[...], wu_ref[...], preferred_element_type=F32)
    h = jnp.square(jnp.maximum(h, 0.0)).astype(BF16)
    o_ref[...] += jnp.dot(h, wd_ref[...], preferred_element_type=F32)

    if final_norm:
        @pl.when(f == pl.num_programs(1) - 1)
        def _():
            o_ref[...] = _rms(o_ref[...], gf_ref[...])


def _mlp(x, g, w_up, w_down, g_final, final_norm):
    m, d = x.shape
    ff = w_up.shape[1]
    tm = min(m, ROW_TILE)
    tf = FF_TILE
    return pl.pallas_call(
        functools.partial(_mlp_kernel, final_norm=final_norm),
        grid=(m // tm, ff // tf),
        in_specs=[pl.BlockSpec((tm, d), lambda i, f: (i, 0)),
                  pl.BlockSpec((1, d), lambda i, f: (0, 0)),
                  pl.BlockSpec((d, tf), lambda i, f: (0, f)),
                  pl.BlockSpec((tf, d), lambda i, f: (f, 0)),
                  pl.BlockSpec((1, d), lambda i, f: (0, 0))],
        out_specs=pl.BlockSpec((tm, d), lambda i, f: (i, 0)),
        out_shape=jax.ShapeDtypeStruct((m, d), F32),
        scratch_shapes=[pltpu.VMEM((tm, d), BF16)],
        compiler_params=_params("parallel", "arbitrary"),
        name="mlp",
    )(x, g.reshape(1, d), w_up, w_down, g_final.reshape(1, d))


def _sigmoid(x):
    return 1.0 / (1.0 + jnp.exp(-x))


def _gelu_tanh(x):
    c = math.sqrt(2.0 / math.pi)
    return x * (0.5 * (1.0 + jnp.tanh(c * (x + 0.044715 * (x * x * x)))))


def _rg_kernel(xb_ref, gb_ref, cbuf_ref, h0_ref, cw_ref, cb_ref, wa_ref, ba_ref, wi_ref, bi_ref,
               lam_ref, y_ref, cout_ref, hout_ref, win_ref, *, seq, tt):
    win_ref[0:CONV_PAD, :] = jnp.zeros((CONV_PAD, RG_BLOCK), F32)
    win_ref[CONV_PAD - (CONV_W - 1):CONV_PAD, :] = cbuf_ref[...]
    win_ref[CONV_PAD:CONV_PAD + seq, :] = xb_ref[...]
    cout_ref[...] = win_ref[CONV_PAD + seq - (CONV_W - 1):CONV_PAD + seq, :]

    cw = cw_ref[...]
    cb = cb_ref[...]
    wa = wa_ref[...]
    wi = wi_ref[...]
    ba = ba_ref[...]
    bi = bi_ref[...]
    neg_lam = -lam_ref[...]
    log_a_scale = -LRU_C * (jnp.maximum(neg_lam, 0.0) + jnp.log(1.0 + jnp.exp(-jnp.abs(neg_lam))))

    def chunk(base, h):
        w = win_ref[pl.ds(base, tt + CONV_PAD), :]
        u = cb
        for k in range(CONV_W):
            off = CONV_PAD - (CONV_W - 1) + k
            u = u + w[off:off + tt] * cw[k:k + 1]
        ub = u.astype(BF16)
        r = _sigmoid(jnp.dot(ub, wa, preferred_element_type=F32) + ba)
        i = _sigmoid(jnp.dot(ub, wi, preferred_element_type=F32) + bi)
        a = jnp.exp(log_a_scale * r)
        b = jnp.sqrt(1.0 - a * a) * (i * u)
        if tt <= SUBLANES:
            rows = []
            for t in range(tt):
                h = a[t:t + 1] * h + b[t:t + 1]
                rows.append(h)
            hseq = jnp.concatenate(rows, axis=0)
        else:
            row = lax.broadcasted_iota(jnp.int32, (tt, RG_BLOCK), 0)
            d = 1
            while d < tt:
                keep = row >= d
                a_sh = jnp.where(keep, pltpu.roll(a, d, 0), 1.0)
                b_sh = jnp.where(keep, pltpu.roll(b, d, 0), 0.0)
                b = a * b_sh + b
                a = a * a_sh
                d *= 2
            hseq = a * h + b
            h = hseq[tt - 1:tt]
        gate = _gelu_tanh(gb_ref[pl.ds(base, tt), :])
        y_ref[pl.ds(base, tt), :] = (hseq * gate).astype(y_ref.dtype)
        return h

    h0 = h0_ref[...]
    if seq == tt:
        h_last = chunk(0, h0)
    else:
        h_last = lax.fori_loop(
            0, seq // tt, lambda c, h: chunk(pl.multiple_of(c * tt, tt), h), h0)
    hout_ref[...] = h_last


def _rg_mix(xz, conv_buf, h0, conv_w, conv_b, w_a, b_a, w_i, b_i, lam, y_dtype):
    bsz, seq, _ = xz.shape
    tt = min(seq, SCAN_TILE)
    nb = RG_BLOCKS
    vec = lambda: pl.BlockSpec((1, RG_BLOCK), lambda b, j: (0, j))
    win_rows = -(-(seq + CONV_PAD) // SUBLANES) * SUBLANES
    return pl.pallas_call(
        functools.partial(_rg_kernel, seq=seq, tt=tt),
        grid=(bsz, nb),
        in_specs=[pl.BlockSpec((None, seq, RG_BLOCK), lambda b, j: (b, 0, j)),
                  pl.BlockSpec((None, seq, RG_BLOCK), lambda b, j: (b, 0, nb + j)),
                  pl.BlockSpec((None, CONV_W - 1, RG_BLOCK), lambda b, j: (b, 0, j)),
                  pl.BlockSpec((None, 1, RG_BLOCK), lambda b, j: (b, 0, j)),
                  pl.BlockSpec((CONV_W, RG_BLOCK), lambda b, j: (0, j)),
                  vec(),
                  pl.BlockSpec((None, RG_BLOCK, RG_BLOCK), lambda b, j: (j, 0, 0)),
                  vec(),
                  pl.BlockSpec((None, RG_BLOCK, RG_BLOCK), lambda b, j: (j, 0, 0)),
                  vec(),
                  vec()],
        out_specs=[pl.BlockSpec((None, seq, RG_BLOCK), lambda b, j: (b, 0, j)),
                   pl.BlockSpec((None, CONV_W - 1, RG_BLOCK), lambda b, j: (b, 0, j)),
                   pl.BlockSpec((None, 1, RG_BLOCK), lambda b, j: (b, 0, j))],
        out_shape=[jax.ShapeDtypeStruct((bsz, seq, D_RNN), y_dtype),
                   jax.ShapeDtypeStruct((bsz, CONV_W - 1, D_RNN), F32),
                   jax.ShapeDtypeStruct((bsz, 1, D_RNN), F32)],
        scratch_shapes=[pltpu.VMEM((win_rows, RG_BLOCK), F32)],
        compiler_params=_params("parallel", "parallel"),
        name="rg_mix",
    )(xz, xz, conv_buf, h0.reshape(bsz, 1, D_RNN), conv_w, conv_b.reshape(1, D_RNN),
      w_a, b_a.reshape(1, D_RNN), w_i, b_i.reshape(1, D_RNN), lam.reshape(1, D_RNN))


def _diff_lambda(lq1_ref, lk1_ref, lq2_ref, lk2_ref, lam_init):
    s1 = jnp.sum(lq1_ref[...] * lk1_ref[...], axis=-1, keepdims=True)
    s2 = jnp.sum(lq2_ref[...] * lk2_ref[...], axis=-1, keepdims=True)
    return jnp.exp(s1) - jnp.exp(s2) + lam_init


def _head_norm(o, subln, lam_init):
    on = o * lax.rsqrt(jnp.mean(o * o, axis=-1, keepdims=True) + EPS)
    return on * subln * (1.0 - lam_init)


_NT = (((1,), (1,)), ((), ()))


def _prompt_attn_kernel(q_ref, k_ref, v_ref, lq1_ref, lk1_ref, lq2_ref, lk2_ref, subln_ref,
                        o_ref, kb_ref, vb_ref, m_ref, l_ref, acc_ref, *, tq, lam_init):
    qi = pl.program_id(2)

    @pl.when(qi == 0)
    def _():
        kb_ref[...] = k_ref[...].astype(BF16)
        vb_ref[...] = v_ref[...].astype(BF16)

    m_ref[...] = jnp.full(m_ref.shape, -jnp.inf, F32)
    l_ref[...] = jnp.zeros(l_ref.shape, F32)
    acc_ref[...] = jnp.zeros(acc_ref.shape, F32)

    q = q_ref[...]
    q1, q2 = q[:, :QK_DIM], q[:, QK_DIM:]

    def block(j, masked):
        start = pl.multiple_of(j * tq, tq)
        kj = kb_ref[pl.ds(start, tq), :]
        vj = vb_ref[pl.ds(start, tq), :]
        s1 = lax.dot_general(q1, kj[:, :QK_DIM], _NT, preferred_element_type=F32)
        s2 = lax.dot_general(q2, kj[:, QK_DIM:], _NT, preferred_element_type=F32)
        s = jnp.concatenate([s1, s2], axis=0)
        if masked:
            row = lax.broadcasted_iota(jnp.int32, (tq, tq), 0)
            col = lax.broadcasted_iota(jnp.int32, (tq, tq), 1)
            keep = col <= row
            s = jnp.where(jnp.concatenate([keep, keep], axis=0), s, -jnp.inf)
        m_prev = m_ref[...]
        m_new = jnp.maximum(m_prev, jnp.max(s, axis=-1, keepdims=True))
        alpha = jnp.exp(m_prev - m_new)
        p = jnp.exp(s - m_new)
        l_ref[...] = alpha * l_ref[...] + jnp.sum(p, axis=-1, keepdims=True)
        acc_ref[...] = alpha * acc_ref[...] + jnp.dot(p.astype(BF16), vj,
                                                      preferred_element_type=F32)
        m_ref[...] = m_new

    def body(j, carry):
        block(j, False)
        return carry

    lax.fori_loop(0, qi, body, 0)
    block(qi, True)

    lam = _diff_lambda(lq1_ref, lk1_ref, lq2_ref, lk2_ref, lam_init)
    pn = acc_ref[...] / l_ref[...]
    o = pn[:tq] - lam * pn[tq:]
    o_ref[...] = _head_norm(o, subln_ref[...], lam_init).astype(o_ref.dtype)


def _prompt_attn(q, k, v, lq1, lk1, lq2, lk2, subln, lam_init, bsz, seq):
    m = bsz * seq
    tq = Q_TILE
    nq = seq // tq
    lvec = lambda: pl.BlockSpec((1, QK_DIM), lambda b, h, i: (0, 0))
    return pl.pallas_call(
        functools.partial(_prompt_attn_kernel, tq=tq, lam_init=lam_init),
        grid=(bsz, N_HEADS, nq),
        in_specs=[pl.BlockSpec((tq, V_DIM), lambda b, h, i: (b * nq + i, h)),
                  pl.BlockSpec((seq, V_DIM), lambda b, h, i: (b, h)),
                  pl.BlockSpec((seq, V_DIM), lambda b, h, i: (b, h)),
                  lvec(), lvec(), lvec(), lvec(),
                  pl.BlockSpec((1, V_DIM), lambda b, h, i: (0, 0))],
        out_specs=pl.BlockSpec((tq, V_DIM), lambda b, h, i: (b * nq + i, h)),
        out_shape=jax.ShapeDtypeStruct((m, V_W), BF16),
        scratch_shapes=[pltpu.VMEM((seq, V_DIM), BF16),
                        pltpu.VMEM((seq, V_DIM), BF16),
                        pltpu.VMEM((2 * tq, 1), F32),
                        pltpu.VMEM((2 * tq, 1), F32),
                        pltpu.VMEM((2 * tq, V_DIM), F32)],
        compiler_params=_params("parallel", "parallel", "arbitrary"),
        name="prompt_attn",
    )(q, k, v, lq1.reshape(1, -1), lk1.reshape(1, -1), lq2.reshape(1, -1), lk2.reshape(1, -1),
      subln.reshape(1, -1))


K_ROWS = PAGE_SIZE * N_HEADS * 2
V_ROWS = PAGE_SIZE * N_HEADS


def _sample_attn_kernel(pt_ref, q_ref, knew_ref, vnew_ref, lq1_ref, lk1_ref, lq2_ref, lk2_ref,
                        subln_ref, *refs, n_pages, dec_seq, lam_init):
    del pt_ref
    k_refs, v_refs = refs[:n_pages], refs[n_pages:2 * n_pages]
    o_ref, qb_ref, bias_ref, m_ref, l_ref, acc_ref = refs[2 * n_pages:]
    g = pl.program_id(1)
    n_q = N_HEADS * dec_seq

    @pl.when(g == 0)
    def _():
        qb_ref[...] = (q_ref[...] * QK_SCALE).astype(BF16)
        row = lax.broadcasted_iota(jnp.int32, bias_ref.shape, 0)
        col = lax.broadcasted_iota(jnp.int32, bias_ref.shape, 1)
        same_head = (row % n_q) // dec_seq == col % N_HEADS
        bias_ref[...] = jnp.where(same_head, 0.0, -jnp.inf)
        m_ref[...] = jnp.full(m_ref.shape, -jnp.inf, F32)
        l_ref[...] = jnp.zeros(l_ref.shape, F32)
        acc_ref[...] = jnp.zeros(acc_ref.shape, F32)

    def attend(k_pages, v_pages, causal):
        bias = bias_ref[...]
        cols = []
        for kp in k_pages:
            per_map = []
            for mp in range(2):
                km = kp[pl.ds(mp, V_ROWS, stride=2), :].astype(BF16)
                per_map.append(lax.dot_general(qb_ref[mp], km, _NT, preferred_element_type=F32))
            cols.append(jnp.concatenate(per_map, axis=0) + bias)
        s = jnp.concatenate(cols, axis=1)
        if causal:
            t = lax.broadcasted_iota(jnp.int32, s.shape, 0) % dec_seq
            tok = lax.broadcasted_iota(jnp.int32, s.shape, 1) // N_HEADS
            s = jnp.where(tok <= t, s, -jnp.inf)
        m_prev = m_ref[...]
        m_new = jnp.maximum(m_prev, jnp.max(s, axis=-1, keepdims=True))
        alpha = jnp.exp(m_prev - m_new)
        p = jnp.exp(s - m_new)
        l_ref[...] = alpha * l_ref[...] + jnp.sum(p, axis=-1, keepdims=True)
        pb = p.astype(BF16)
        pv = None
        for i, vp in enumerate(v_pages):
            contrib = jnp.dot(pb[:, i * V_ROWS:(i + 1) * V_ROWS], vp[...].astype(BF16),
                              preferred_element_type=F32)
            pv = contrib if pv is None else pv + contrib
        acc_ref[...] = alpha * acc_ref[...] + pv
        m_ref[...] = m_new

    attend(k_refs, v_refs, False)

    @pl.when(g == pl.num_programs(1) - 1)
    def _():
        attend([knew_ref], [vnew_ref], True)
        lam = _diff_lambda(lq1_ref, lk1_ref, lq2_ref, lk2_ref, lam_init)
        pn = acc_ref[...] / l_ref[...]
        o = pn[:n_q] - lam * pn[n_q:]
        o_ref[...] = _head_norm(o, subln_ref[...], lam_init)


def _sample_attn(q, k_new, v_new, cache_k, cache_v, layer, page_table, lq1, lk1, lq2, lk2, subln,
                 lam_init):
    bsz, dec_seq, _ = q.shape
    n_pages_total = page_table.shape[1]
    gp = PAGES_PER_STEP
    n_q = N_HEADS * dec_seq
    q_arr = q.reshape(bsz, dec_seq, N_HEADS, 2, QK_DIM).transpose(0, 3, 2, 1, 4)
    q_arr = q_arr.reshape(bsz, 2, n_q, QK_DIM)
    k_rows, v_rows = dec_seq * N_HEADS * 2, dec_seq * N_HEADS
    k_page = jnp.pad(k_new.reshape(bsz, k_rows, QK_DIM), ((0, 0), (0, K_ROWS - k_rows), (0, 0)))
    v_page = jnp.pad(v_new.reshape(bsz, v_rows, V_DIM), ((0, 0), (0, V_ROWS - v_rows), (0, 0)))

    def page_spec(i, rows, width):
        return pl.BlockSpec((None, None, rows, width),
                            lambda b, g, pt, i=i: (layer, pt[b, g * gp + i], 0, 0))

    lvec = lambda: pl.BlockSpec((1, QK_DIM), lambda b, g, pt: (0, 0))
    grid_spec = pltpu.PrefetchScalarGridSpec(
        num_scalar_prefetch=1,
        grid=(bsz, n_pages_total // gp),
        in_specs=[pl.BlockSpec((None, 2, n_q, QK_DIM), lambda b, g, pt: (b, 0, 0, 0)),
                  pl.BlockSpec((None, K_ROWS, QK_DIM), lambda b, g, pt: (b, 0, 0)),
                  pl.BlockSpec((None, V_ROWS, V_DIM), lambda b, g, pt: (b, 0, 0)),
                  lvec(), lvec(), lvec(), lvec(),
                  pl.BlockSpec((1, V_DIM), lambda b, g, pt: (0, 0))]
                 + [page_spec(i, K_ROWS, QK_DIM) for i in range(gp)]
                 + [page_spec(i, V_ROWS, V_DIM) for i in range(gp)],
        out_specs=pl.BlockSpec((None, n_q, V_DIM), lambda b, g, pt: (b, 0, 0)),
        scratch_shapes=[pltpu.VMEM((2, n_q, QK_DIM), BF16),
                        pltpu.VMEM((2 * n_q, V_ROWS), F32),
                        pltpu.VMEM((2 * n_q, 1), F32),
                        pltpu.VMEM((2 * n_q, 1), F32),
                        pltpu.VMEM((2 * n_q, V_DIM), F32)],
    )
    out = pl.pallas_call(
        functools.partial(_sample_attn_kernel, n_pages=gp, dec_seq=dec_seq, lam_init=lam_init),
        grid_spec=grid_spec,
        out_shape=jax.ShapeDtypeStruct((bsz, n_q, V_DIM), F32),
        compiler_params=_params("parallel", "arbitrary"),
        name="sample_attn",
    )(page_table, q_arr, k_page, v_page, lq1.reshape(1, -1), lk1.reshape(1, -1), lq2.reshape(1, -1),
      lk2.reshape(1, -1), subln.reshape(1, -1), *([cache_k] * gp), *([cache_v] * gp))
    return out.reshape(bsz, N_HEADS, dec_seq, V_DIM).transpose(0, 2, 1, 3).reshape(bsz, dec_seq, V_W)


def kernel(x_prompt, x_sample, cache_k, cache_v, state_conv, state_h, page_table, norm_mix, norm_mlp, norm_final, rg_w_in, rg_conv_w, rg_conv_b, rg_w_a, rg_b_a, rg_w_i, rg_b_i, rg_lambda, rg_w_out, da_w_qkv, da_lq1, da_lk1, da_lq2, da_lk2, da_subln, da_w_o, mlp_w_up, mlp_w_down):
    bsz, seq, d = x_prompt.shape
    dbsz, dseq, _ = x_sample.shape
    xp = x_prompt.reshape(bsz * seq, d)
    xs = x_sample.reshape(dbsz * dseq, d)
    n_layers_b, n_pool = cache_k.shape[:2]
    ck = cache_k.reshape(n_layers_b, n_pool, K_ROWS, QK_DIM)
    cv = cache_v.reshape(n_layers_b, n_pool, V_ROWS, V_DIM)

    w_in, w_out = rg_w_in.astype(BF16), rg_w_out.astype(BF16)
    w_a, w_i = rg_w_a.astype(BF16), rg_w_i.astype(BF16)
    w_qkv, w_o = da_w_qkv.astype(BF16), da_w_o.astype(BF16)
    w_up, w_down = mlp_w_up.astype(BF16), mlp_w_down.astype(BF16)

    kp_l, vp_l, cp_l, hp_l = [], [], [], []
    ks_l, vs_l, cs_l, hs_l = [], [], [], []
    for i in range(DEPTH):
        j = i // 2
        if i % 2 == 0:
            rg = (rg_conv_w[j], rg_conv_b[j], w_a[j], rg_b_a[j], w_i[j], rg_b_i[j], rg_lambda[j])
            (xz_p,) = _norm_proj(xp, norm_mix[i], w_in[j], 2 * D_RNN, [F32], [1.0])
            (xz_s,) = _norm_proj(xs, norm_mix[i], w_in[j], 2 * D_RNN, [F32], [1.0])
            yp, cbp, hlp = _rg_mix(xz_p.reshape(bsz, seq, 2 * D_RNN),
                                   jnp.zeros((bsz, CONV_W - 1, D_RNN), F32),
                                   jnp.zeros((bsz, D_RNN), F32), *rg, BF16)
            ys, cbs, hls = _rg_mix(xz_s.reshape(dbsz, dseq, 2 * D_RNN), state_conv[j], state_h[j],
                                   *rg, F32)
            cp_l.append(cbp); hp_l.append(hlp.reshape(bsz, D_RNN))
            cs_l.append(cbs); hs_l.append(hls.reshape(dbsz, D_RNN))
            xp = _proj_res(yp.reshape(bsz * seq, D_RNN), w_out[j], xp)
            xs = _proj_res(ys.reshape(dbsz * dseq, D_RNN), w_out[j], xs)
        else:
            lam_init = _lambda_init(i)
            lvecs = (da_lq1[j], da_lk1[j], da_lq2[j], da_lk2[j], da_subln[j])
            qp, kp, vp = _norm_proj(xp, norm_mix[i], w_qkv[j], QK_W, [BF16, F32, F32],
                                    [QK_SCALE, 1.0, 1.0])
            qs, kn, vn = _norm_proj(xs, norm_mix[i], w_qkv[j], QK_W, [F32, F32, F32],
                                    [1.0, 1.0, 1.0])
            op = _prompt_attn(qp, kp, vp, *lvecs, lam_init, bsz, seq)
            os_ = _sample_attn(qs.reshape(dbsz, dseq, QK_W), kn.reshape(dbsz, dseq, QK_W),
                               vn.reshape(dbsz, dseq, V_W), ck, cv, j, page_table, *lvecs, lam_init)
            kp_l.append(kp.reshape(bsz, seq, N_HEADS, 2, QK_DIM))
            vp_l.append(vp.reshape(bsz, seq, N_HEADS, V_DIM))
            ks_l.append(kn.reshape(dbsz, dseq, N_HEADS, 2, QK_DIM))
            vs_l.append(vn.reshape(dbsz, dseq, N_HEADS, V_DIM))
            xp = _proj_res(op, w_o[j], xp)
            xs = _proj_res(os_.reshape(dbsz * dseq, V_W), w_o[j], xs)
        last = i == DEPTH - 1
        xp = _mlp(xp, norm_mlp[i], w_up[i], w_down[i], norm_final, last)
        xs = _mlp(xs, norm_mlp[i], w_up[i], w_down[i], norm_final, last)
    return (xp.reshape(bsz, seq, d), xs.reshape(dbsz, dseq, d),
            jnp.stack(kp_l), jnp.stack(vp_l), jnp.stack(cp_l), jnp.stack(hp_l),
            jnp.stack(ks_l), jnp.stack(vs_l), jnp.stack(cs_l), jnp.stack(hs_l))
```

```python
import functools
import math

import jax
import jax.numpy as jnp
from jax import lax
from jax.experimental import pallas as pl
from jax.experimental.pallas import tpu as pltpu

F32 = jnp.float32
BF16 = jnp.bfloat16

D_MODEL = 2048
DEPTH = 4
PAGE_SIZE = 128
D_RNN = D_MODEL
RG_BLOCK = 256
RG_BLOCKS = D_RNN // RG_BLOCK
CONV_W = 4
LRU_C = 8.0
QK_DIM = 128
V_DIM = 2 * QK_DIM
N_HEADS = D_MODEL // V_DIM
QK_W = N_HEADS * 2 * QK_DIM
V_W = N_HEADS * V_DIM
D_FF = 4 * D_MODEL
EPS = 1e-6
QK_SCALE = QK_DIM ** -0.5

VMEM_LIMIT_BYTES = 56 * 1024 * 1024
SUBLANES = 8
LANES = 128
LOG2_E = math.log2(math.e)

ROW_TILE = 1024
RES_ROW_TILE = 512
COL_TILE = 1024
GROUP_COL_TILE = 512
FF_TILE = 512
Q_TILE = 512
SCAN_TILE = 256
PAGES_PER_STEP = 4
CONV_PAD = SUBLANES


def _params(*semantics):
    return pltpu.CompilerParams(dimension_semantics=semantics,
                                vmem_limit_bytes=VMEM_LIMIT_BYTES)


def _rms(x, g):
    return x * lax.rsqrt(jnp.mean(x * x, axis=-1, keepdims=True) + EPS) * g


def _lambda_init(layer_idx):
    return 0.8 - 0.6 * math.exp(-0.3 * layer_idx)


def _norm_proj_kernel(x_ref, g_ref, *refs, n_out, scales):
    w_refs, o_refs, xn_ref = refs[:n_out], refs[n_out:2 * n_out], refs[2 * n_out]

    @pl.when(pl.program_id(1) == 0)
    def _():
        xn_ref[...] = _rms(x_ref[...], g_ref[...]).astype(BF16)

    xn = xn_ref[...]
    for w_ref, o_ref, scale in zip(w_refs, o_refs, scales):
        acc = jnp.dot(xn, w_ref[...], preferred_element_type=F32)
        if scale != 1.0:
            acc = acc * scale
        o_ref[...] = acc.astype(o_ref.dtype)


def _norm_proj(x, g, w, layer, group_width, out_dtypes, scales):
    m, d = x.shape
    n_out = len(out_dtypes)
    assert w.shape[1:] == (d, n_out * group_width)
    tm = min(m, ROW_TILE)
    tn = COL_TILE if n_out == 1 else GROUP_COL_TILE
    nj = group_width // tn
    in_specs = [pl.BlockSpec((tm, d), lambda i, j: (i, 0)),
                pl.BlockSpec((1, d), lambda i, j: (0, 0))]
    for k in range(n_out):
        in_specs.append(pl.BlockSpec((None, d, tn), lambda i, j, k=k: (layer, 0, k * nj + j)))
    return pl.pallas_call(
        functools.partial(_norm_proj_kernel, n_out=n_out, scales=tuple(scales)),
        grid=(m // tm, nj),
        in_specs=in_specs,
        out_specs=[pl.BlockSpec((tm, tn), lambda i, j: (i, j))] * n_out,
        out_shape=[jax.ShapeDtypeStruct((m, group_width), dt) for dt in out_dtypes],
        scratch_shapes=[pltpu.VMEM((tm, d), BF16)],
        compiler_params=_params("parallel", "arbitrary"),
        name="norm_proj",
    )(x, g.reshape(1, d), *([w] * n_out))


def _proj_res_kernel(a_ref, w_ref, r_ref, o_ref):
    o_ref[...] = r_ref[...] + jnp.dot(a_ref[...].astype(BF16), w_ref[...],
                                      preferred_element_type=F32)


def _proj_res(a, w, layer, res):
    m, k = a.shape
    n = w.shape[2]
    tm = min(m, RES_ROW_TILE)
    return pl.pallas_call(
        _proj_res_kernel,
        grid=(m // tm,),
        in_specs=[pl.BlockSpec((tm, k), lambda i: (i, 0)),
                  pl.BlockSpec((None, k, n), lambda i: (layer, 0, 0)),
                  pl.BlockSpec((tm, n), lambda i: (i, 0))],
        out_specs=pl.BlockSpec((tm, n), lambda i: (i, 0)),
        out_shape=jax.ShapeDtypeStruct((m, n), F32),
        compiler_params=_params("parallel"),
        name="proj_res",
    )(a, w, res)


def _mlp_kernel(x_ref, g_ref, wu_ref, wd_ref, gf_ref, o_ref, xn_ref, *, final_norm):
    f = pl.program_id(1)

    @pl.when(f == 0)
    def _():
        x = x_ref[...]
        xn_ref[...] = _rms(x, g_ref[...]).astype(BF16)
        o_ref[...] = x

    h = jnp.dot(xn_ref[...], wu_ref[...], preferred_element_type=F32)
    h = jnp.square(jnp.maximum(h, 0.0)).astype(BF16)
    o_ref[...] += jnp.dot(h, wd_ref[...], preferred_element_type=F32)

    if final_norm:
        @pl.when(f == pl.num_programs(1) - 1)
        def _():
            o_ref[...] = _rms(o_ref[...], gf_ref[...])


def _mlp(x, g, w_up, w_down, layer, g_final, final_norm):
    m, d = x.shape
    ff = w_up.shape[2]
    tm = min(m, ROW_TILE)
    tf = FF_TILE
    return pl.pallas_call(
        functools.partial(_mlp_kernel, final_norm=final_norm),
        grid=(m // tm, ff // tf),
        in_specs=[pl.BlockSpec((tm, d), lambda i, f: (i, 0)),
                  pl.BlockSpec((1, d), lambda i, f: (0, 0)),
                  pl.BlockSpec((None, d, tf), lambda i, f: (layer, 0, f)),
                  pl.BlockSpec((None, tf, d), lambda i, f: (layer, f, 0)),
                  pl.BlockSpec((1, d), lambda i, f: (0, 0))],
        out_specs=pl.BlockSpec((tm, d), lambda i, f: (i, 0)),
        out_shape=jax.ShapeDtypeStruct((m, d), F32),
        scratch_shapes=[pltpu.VMEM((tm, d), BF16)],
        compiler_params=_params("parallel", "arbitrary"),
        name="mlp",
    )(x, g.reshape(1, d), w_up, w_down, g_final.reshape(1, d))


def _sigmoid(x):
    return 1.0 / (1.0 + jnp.exp(-x))


def _gelu_tanh(x):
    c = math.sqrt(2.0 / math.pi)
    return x * (0.5 * (1.0 + jnp.tanh(c * (x + 0.044715 * (x * x * x)))))


def _rg_kernel(xb_ref, gb_ref, cbuf_ref, h0_ref, cw_ref, cb_ref, wa_ref, ba_ref, wi_ref, bi_ref,
               lam_ref, y_ref, cout_ref, hout_ref, win_ref, *, seq, tt):
    win_ref[0:CONV_PAD, :] = jnp.zeros((CONV_PAD, RG_BLOCK), F32)
    win_ref[CONV_PAD - (CONV_W - 1):CONV_PAD, :] = cbuf_ref[...]
    win_ref[CONV_PAD:CONV_PAD + seq, :] = xb_ref[...]
    cout_ref[...] = win_ref[CONV_PAD + seq - (CONV_W - 1):CONV_PAD + seq, :]

    cw = cw_ref[...]
    cb = cb_ref[...]
    wa = wa_ref[...]
    wi = wi_ref[...]
    ba = ba_ref[...]
    bi = bi_ref[...]
    neg_lam = -lam_ref[...]
    log_a_scale = -LRU_C * (jnp.maximum(neg_lam, 0.0) + jnp.log(1.0 + jnp.exp(-jnp.abs(neg_lam))))

    def chunk(base, h):
        w = win_ref[pl.ds(base, tt + CONV_PAD), :]
        u = cb
        for k in range(CONV_W):
            off = CONV_PAD - (CONV_W - 1) + k
            u = u + w[off:off + tt] * cw[k:k + 1]
        ub = u.astype(BF16)
        r = _sigmoid(jnp.dot(ub, wa, preferred_element_type=F32) + ba)
        i = _sigmoid(jnp.dot(ub, wi, preferred_element_type=F32) + bi)
        a = jnp.exp(log_a_scale * r)
        b = jnp.sqrt(1.0 - a * a) * (i * u)
        if tt <= SUBLANES:
            rows = []
            for t in range(tt):
                h = a[t:t + 1] * h + b[t:t + 1]
                rows.append(h)
            hseq = jnp.concatenate(rows, axis=0)
        else:
            sub = lax.broadcasted_iota(jnp.int32, (SUBLANES, RG_BLOCK), 0)
            rows = []
            for g in range(tt // SUBLANES):
                ag = a[g * SUBLANES:(g + 1) * SUBLANES]
                bg = b[g * SUBLANES:(g + 1) * SUBLANES]
                d = 1
                while d < SUBLANES:
                    keep = sub >= d
                    a_sh = jnp.where(keep, pltpu.roll(ag, d, 0), 1.0)
                    b_sh = jnp.where(keep, pltpu.roll(bg, d, 0), 0.0)
                    bg = ag * b_sh + bg
                    ag = ag * a_sh
                    d *= 2
                hg = ag * h + bg
                h = hg[SUBLANES - 1:SUBLANES]
                rows.append(hg)
            hseq = jnp.concatenate(rows, axis=0)
        gate = _gelu_tanh(gb_ref[pl.ds(base, tt), :])
        y_ref[pl.ds(base, tt), :] = (hseq * gate).astype(y_ref.dtype)
        return h

    h0 = h0_ref[...]
    if seq == tt:
        h_last = chunk(0, h0)
    else:
        h_last = lax.fori_loop(
            0, seq // tt, lambda c, h: chunk(pl.multiple_of(c * tt, tt), h), h0)
    hout_ref[...] = h_last


def _rg_mix(xz, conv_buf, h0, conv_w, conv_b, w_a, b_a, w_i, b_i, lam, y_dtype):
    bsz, seq, _ = xz.shape
    tt = min(seq, SCAN_TILE)
    nb = RG_BLOCKS
    vec = lambda: pl.BlockSpec((1, RG_BLOCK), lambda b, j: (0, j))
    win_rows = -(-(seq + CONV_PAD) // SUBLANES) * SUBLANES
    return pl.pallas_call(
        functools.partial(_rg_kernel, seq=seq, tt=tt),
        grid=(bsz, nb),
        in_specs=[pl.BlockSpec((None, seq, RG_BLOCK), lambda b, j: (b, 0, j)),
                  pl.BlockSpec((None, seq, RG_BLOCK), lambda b, j: (b, 0, nb + j)),
                  pl.BlockSpec((None, CONV_W - 1, RG_BLOCK), lambda b, j: (b, 0, j)),
                  pl.BlockSpec((None, 1, RG_BLOCK), lambda b, j: (b, 0, j)),
                  pl.BlockSpec((CONV_W, RG_BLOCK), lambda b, j: (0, j)),
                  vec(),
                  pl.BlockSpec((None, RG_BLOCK, RG_BLOCK), lambda b, j: (j, 0, 0)),
                  vec(),
                  pl.BlockSpec((None, RG_BLOCK, RG_BLOCK), lambda b, j: (j, 0, 0)),
                  vec(),
                  vec()],
        out_specs=[pl.BlockSpec((None, seq, RG_BLOCK), lambda b, j: (b, 0, j)),
                   pl.BlockSpec((None, CONV_W - 1, RG_BLOCK), lambda b, j: (b, 0, j)),
                   pl.BlockSpec((None, 1, RG_BLOCK), lambda b, j: (b, 0, j))],
        out_shape=[jax.ShapeDtypeStruct((bsz, seq, D_RNN), y_dtype),
                   jax.ShapeDtypeStruct((bsz, CONV_W - 1, D_RNN), F32),
                   jax.ShapeDtypeStruct((bsz, 1, D_RNN), F32)],
        scratch_shapes=[pltpu.VMEM((win_rows, RG_BLOCK), F32)],
        compiler_params=_params("parallel", "parallel"),
        name="rg_mix",
    )(xz, xz, conv_buf, h0.reshape(bsz, 1, D_RNN), conv_w, conv_b.reshape(1, D_RNN),
      w_a, b_a.reshape(1, D_RNN), w_i, b_i.reshape(1, D_RNN), lam.reshape(1, D_RNN))


def _diff_lambda(lq1_ref, lk1_ref, lq2_ref, lk2_ref, lam_init):
    s1 = jnp.sum(lq1_ref[...] * lk1_ref[...], axis=-1, keepdims=True)
    s2 = jnp.sum(lq2_ref[...] * lk2_ref[...], axis=-1, keepdims=True)
    return jnp.exp(s1) - jnp.exp(s2) + lam_init


def _head_norm(o, subln, lam_init):
    on = o * lax.rsqrt(jnp.mean(o * o, axis=-1, keepdims=True) + EPS)
    return on * subln * (1.0 - lam_init)


_NT = (((1,), (1,)), ((), ()))


def _prompt_attn_kernel(q_ref, k_ref, v_ref, lq1_ref, lk1_ref, lq2_ref, lk2_ref, subln_ref,
                        o_ref, kb_ref, vb_ref, m_ref, l_ref, acc_ref, *, tq, lam_init):
    qi = pl.program_id(2)

    @pl.when(qi == 0)
    def _():
        kb_ref[...] = k_ref[...].astype(BF16)
        vb_ref[...] = v_ref[...].astype(BF16)

    m_ref[...] = jnp.full(m_ref.shape, -jnp.inf, F32)
    l_ref[...] = jnp.zeros(l_ref.shape, F32)
    acc_ref[...] = jnp.zeros(acc_ref.shape, F32)

    q = q_ref[...]
    q1, q2 = q[:, :QK_DIM], q[:, QK_DIM:]

    def block(j, masked):
        start = pl.multiple_of(j * tq, tq)
        kj = kb_ref[pl.ds(start, tq), :]
        vj = vb_ref[pl.ds(start, tq), :]
        s1 = lax.dot_general(q1, kj[:, :QK_DIM], _NT, preferred_element_type=F32)
        s2 = lax.dot_general(q2, kj[:, QK_DIM:], _NT, preferred_element_type=F32)
        s = jnp.concatenate([s1, s2], axis=0)
        if masked:
            row = lax.broadcasted_iota(jnp.int32, (tq, tq), 0)
            col = lax.broadcasted_iota(jnp.int32, (tq, tq), 1)
            keep = col <= row
            s = jnp.where(jnp.concatenate([keep, keep], axis=0), s, -jnp.inf)
        m_prev = m_ref[...]
        m_new = jnp.maximum(m_prev, jnp.max(s, axis=-1, keepdims=True))
        alpha = jnp.exp2(m_prev - m_new)
        p = jnp.exp2(s - pltpu.repeat(m_new, tq // LANES, axis=1))
        p_lanes = p[:, :LANES]
        for c in range(1, tq // LANES):
            p_lanes = p_lanes + p[:, c * LANES:(c + 1) * LANES]
        l_ref[...] = alpha * l_ref[...] + p_lanes
        acc_ref[...] = (pltpu.repeat(alpha, V_DIM // LANES, axis=1) * acc_ref[...]
                        + jnp.dot(p.astype(BF16), vj, preferred_element_type=F32))
        m_ref[...] = m_new

    def body(j, carry):
        block(j, False)
        return carry

    lax.fori_loop(0, qi, body, 0)
    block(qi, True)

    lam = _diff_lambda(lq1_ref, lk1_ref, lq2_ref, lk2_ref, lam_init)
    pn = acc_ref[...] / jnp.sum(l_ref[...], axis=-1, keepdims=True)
    o = pn[:tq] - lam * pn[tq:]
    o_ref[...] = _head_norm(o, subln_ref[...], lam_init).astype(o_ref.dtype)


def _prompt_attn(q, k, v, lq1, lk1, lq2, lk2, subln, lam_init, bsz, seq):
    m = bsz * seq
    tq = Q_TILE
    nq = seq // tq
    lvec = lambda: pl.BlockSpec((1, QK_DIM), lambda b, h, i: (0, 0))
    return pl.pallas_call(
        functools.partial(_prompt_attn_kernel, tq=tq, lam_init=lam_init),
        grid=(bsz, N_HEADS, nq),
        in_specs=[pl.BlockSpec((tq, V_DIM), lambda b, h, i: (b * nq + i, h)),
                  pl.BlockSpec((seq, V_DIM), lambda b, h, i: (b, h)),
                  pl.BlockSpec((seq, V_DIM), lambda b, h, i: (b, h)),
                  lvec(), lvec(), lvec(), lvec(),
                  pl.BlockSpec((1, V_DIM), lambda b, h, i: (0, 0))],
        out_specs=pl.BlockSpec((tq, V_DIM), lambda b, h, i: (b * nq + i, h)),
        out_shape=jax.ShapeDtypeStruct((m, V_W), BF16),
        scratch_shapes=[pltpu.VMEM((seq, V_DIM), BF16),
                        pltpu.VMEM((seq, V_DIM), BF16),
                        pltpu.VMEM((2 * tq, LANES), F32),
                        pltpu.VMEM((2 * tq, LANES), F32),
                        pltpu.VMEM((2 * tq, V_DIM), F32)],
        compiler_params=_params("parallel", "parallel", "arbitrary"),
        name="prompt_attn",
    )(q, k, v, lq1.reshape(1, -1), lk1.reshape(1, -1), lq2.reshape(1, -1), lk2.reshape(1, -1),
      subln.reshape(1, -1))


K_ROWS = PAGE_SIZE * N_HEADS * 2
V_ROWS = PAGE_SIZE * N_HEADS


def _sample_attn_kernel(pt_ref, q_ref, knew_ref, vnew_ref, lq1_ref, lk1_ref, lq2_ref, lk2_ref,
                        subln_ref, *refs, n_pages, dec_seq, lam_init):
    del pt_ref
    k_refs, v_refs = refs[:n_pages], refs[n_pages:2 * n_pages]
    o_ref, qb_ref, bias_ref, m_ref, l_ref, acc_ref = refs[2 * n_pages:]
    g = pl.program_id(1)
    n_q = N_HEADS * dec_seq

    @pl.when(g == 0)
    def _():
        qb_ref[...] = (q_ref[...] * QK_SCALE).astype(BF16)
        row = lax.broadcasted_iota(jnp.int32, bias_ref.shape, 0)
        col = lax.broadcasted_iota(jnp.int32, bias_ref.shape, 1)
        same_head = (row % n_q) // dec_seq == col % N_HEADS
        bias_ref[...] = jnp.where(same_head, 0.0, -jnp.inf)
        m_ref[...] = jnp.full(m_ref.shape, -jnp.inf, F32)
        l_ref[...] = jnp.zeros(l_ref.shape, F32)
        acc_ref[...] = jnp.zeros(acc_ref.shape, F32)

    def attend(k_pages, v_pages, causal):
        bias = bias_ref[...]
        cols = []
        for kp in k_pages:
            per_map = []
            for mp in range(2):
                km = kp[pl.ds(mp, V_ROWS, stride=2), :].astype(BF16)
                per_map.append(lax.dot_general(qb_ref[mp], km, _NT, preferred_element_type=F32))
            cols.append(jnp.concatenate(per_map, axis=0) + bias)
        s = jnp.concatenate(cols, axis=1)
        if causal:
            t = lax.broadcasted_iota(jnp.int32, s.shape, 0) % dec_seq
            tok = lax.broadcasted_iota(jnp.int32, s.shape, 1) // N_HEADS
            s = jnp.where(tok <= t, s, -jnp.inf)
        m_prev = m_ref[...]
        m_new = jnp.maximum(m_prev, jnp.max(s, axis=-1, keepdims=True))
        alpha = jnp.exp(m_prev - m_new)
        p = jnp.exp(s - m_new)
        l_ref[...] = alpha * l_ref[...] + jnp.sum(p, axis=-1, keepdims=True)
        pb = p.astype(BF16)
        pv = None
        for i, vp in enumerate(v_pages):
            contrib = jnp.dot(pb[:, i * V_ROWS:(i + 1) * V_ROWS], vp[...].astype(BF16),
                              preferred_element_type=F32)
            pv = contrib if pv is None else pv + contrib
        acc_ref[...] = alpha * acc_ref[...] + pv
        m_ref[...] = m_new

    attend(k_refs, v_refs, False)

    @pl.when(g == pl.num_programs(1) - 1)
    def _():
        attend([knew_ref], [vnew_ref], True)
        lam = _diff_lambda(lq1_ref, lk1_ref, lq2_ref, lk2_ref, lam_init)
        pn = acc_ref[...] / l_ref[...]
        o = pn[:n_q] - lam * pn[n_q:]
        o_ref[...] = _head_norm(o, subln_ref[...], lam_init)


def _sample_attn(q, k_new, v_new, cache_k, cache_v, layer, page_table, lq1, lk1, lq2, lk2, subln,
                 lam_init):
    bsz, dec_seq, _ = q.shape
    n_pages_total = page_table.shape[1]
    gp = PAGES_PER_STEP
    n_q = N_HEADS * dec_seq
    q_arr = q.reshape(bsz, dec_seq, N_HEADS, 2, QK_DIM).transpose(0, 3, 2, 1, 4)
    q_arr = q_arr.reshape(bsz, 2, n_q, QK_DIM)
    k_rows, v_rows = dec_seq * N_HEADS * 2, dec_seq * N_HEADS
    k_page = jnp.pad(k_new.reshape(bsz, k_rows, QK_DIM), ((0, 0), (0, K_ROWS - k_rows), (0, 0)))
    v_page = jnp.pad(v_new.reshape(bsz, v_rows, V_DIM), ((0, 0), (0, V_ROWS - v_rows), (0, 0)))

    def page_spec(i, rows, width):
        return pl.BlockSpec((None, None, rows, width),
                            lambda b, g, pt, i=i: (layer, pt[b, g * gp + i], 0, 0))

    lvec = lambda: pl.BlockSpec((1, QK_DIM), lambda b, g, pt: (0, 0))
    grid_spec = pltpu.PrefetchScalarGridSpec(
        num_scalar_prefetch=1,
        grid=(bsz, n_pages_total // gp),
        in_specs=[pl.BlockSpec((None, 2, n_q, QK_DIM), lambda b, g, pt: (b, 0, 0, 0)),
                  pl.BlockSpec((None, K_ROWS, QK_DIM), lambda b, g, pt: (b, 0, 0)),
                  pl.BlockSpec((None, V_ROWS, V_DIM), lambda b, g, pt: (b, 0, 0)),
                  lvec(), lvec(), lvec(), lvec(),
                  pl.BlockSpec((1, V_DIM), lambda b, g, pt: (0, 0))]
                 + [page_spec(i, K_ROWS, QK_DIM) for i in range(gp)]
                 + [page_spec(i, V_ROWS, V_DIM) for i in range(gp)],
        out_specs=pl.BlockSpec((None, n_q, V_DIM), lambda b, g, pt: (b, 0, 0)),
        scratch_shapes=[pltpu.VMEM((2, n_q, QK_DIM), BF16),
                        pltpu.VMEM((2 * n_q, V_ROWS), F32),
                        pltpu.VMEM((2 * n_q, 1), F32),
                        pltpu.VMEM((2 * n_q, 1), F32),
                        pltpu.VMEM((2 * n_q, V_DIM), F32)],
    )
    out = pl.pallas_call(
        functools.partial(_sample_attn_kernel, n_pages=gp, dec_seq=dec_seq, lam_init=lam_init),
        grid_spec=grid_spec,
        out_shape=jax.ShapeDtypeStruct((bsz, n_q, V_DIM), F32),
        compiler_params=_params("parallel", "arbitrary"),
        name="sample_attn",
    )(page_table, q_arr, k_page, v_page, lq1.reshape(1, -1), lk1.reshape(1, -1), lq2.reshape(1, -1),
      lk2.reshape(1, -1), subln.reshape(1, -1), *([cache_k] * gp), *([cache_v] * gp))
    return out.reshape(bsz, N_HEADS, dec_seq, V_DIM).transpose(0, 2, 1, 3).reshape(bsz, dec_seq, V_W)


def kernel(x_prompt, x_sample, cache_k, cache_v, state_conv, state_h, page_table, norm_mix, norm_mlp, norm_final, rg_w_in, rg_conv_w, rg_conv_b, rg_w_a, rg_b_a, rg_w_i, rg_b_i, rg_lambda, rg_w_out, da_w_qkv, da_lq1, da_lk1, da_lq2, da_lk2, da_subln, da_w_o, mlp_w_up, mlp_w_down):
    bsz, seq, d = x_prompt.shape
    dbsz, dseq, _ = x_sample.shape
    xp = x_prompt.reshape(bsz * seq, d)
    xs = x_sample.reshape(dbsz * dseq, d)
    n_layers_b, n_pool = cache_k.shape[:2]
    ck = cache_k.reshape(n_layers_b, n_pool, K_ROWS, QK_DIM)
    cv = cache_v.reshape(n_layers_b, n_pool, V_ROWS, V_DIM)

    w_in, w_out = rg_w_in.astype(BF16), rg_w_out.astype(BF16)
    w_a, w_i = rg_w_a.astype(BF16), rg_w_i.astype(BF16)
    w_qkv, w_o = da_w_qkv.astype(BF16), da_w_o.astype(BF16)
    w_up, w_down = mlp_w_up.astype(BF16), mlp_w_down.astype(BF16)

    kp_l, vp_l, cp_l, hp_l = [], [], [], []
    ks_l, vs_l, cs_l, hs_l = [], [], [], []
    for i in range(DEPTH):
        j = i // 2
        if i % 2 == 0:
            rg = (rg_conv_w[j], rg_conv_b[j], w_a[j], rg_b_a[j], w_i[j], rg_b_i[j], rg_lambda[j])
            (xz_p,) = _norm_proj(xp, norm_mix[i], w_in, j, 2 * D_RNN, [F32], [1.0])
            (xz_s,) = _norm_proj(xs, norm_mix[i], w_in, j, 2 * D_RNN, [F32], [1.0])
            yp, cbp, hlp = _rg_mix(xz_p.reshape(bsz, seq, 2 * D_RNN),
                                   jnp.zeros((bsz, CONV_W - 1, D_RNN), F32),
                                   jnp.zeros((bsz, D_RNN), F32), *rg, BF16)
            ys, cbs, hls = _rg_mix(xz_s.reshape(dbsz, dseq, 2 * D_RNN), state_conv[j], state_h[j],
                                   *rg, F32)
            cp_l.append(cbp); hp_l.append(hlp.reshape(bsz, D_RNN))
            cs_l.append(cbs); hs_l.append(hls.reshape(dbsz, D_RNN))
            xp = _proj_res(yp.reshape(bsz * seq, D_RNN), w_out, j, xp)
            xs = _proj_res(ys.reshape(dbsz * dseq, D_RNN), w_out, j, xs)
        else:
            lam_init = _lambda_init(i)
            lvecs = (da_lq1[j], da_lk1[j], da_lq2[j], da_lk2[j], da_subln[j])
            qp, kp, vp = _norm_proj(xp, norm_mix[i], w_qkv, j, QK_W, [BF16, F32, F32],
                                    [QK_SCALE * LOG2_E, 1.0, 1.0])
            qs, kn, vn = _norm_proj(xs, norm_mix[i], w_qkv, j, QK_W, [F32, F32, F32],
                                    [1.0, 1.0, 1.0])
            op = _prompt_attn(qp, kp, vp, *lvecs, lam_init, bsz, seq)
            os_ = _sample_attn(qs.reshape(dbsz, dseq, QK_W), kn.reshape(dbsz, dseq, QK_W),
                               vn.reshape(dbsz, dseq, V_W), ck, cv, j, page_table, *lvecs, lam_init)
            kp_l.append(kp.reshape(bsz, seq, N_HEADS, 2, QK_DIM))
            vp_l.append(vp.reshape(bsz, seq, N_HEADS, V_DIM))
            ks_l.append(kn.reshape(dbsz, dseq, N_HEADS, 2, QK_DIM))
            vs_l.append(vn.reshape(dbsz, dseq, N_HEADS, V_DIM))
            xp = _proj_res(op, w_o, j, xp)
            xs = _proj_res(os_.reshape(dbsz * dseq, V_W), w_o, j, xs)
        last = i == DEPTH - 1
        xp = _mlp(xp, norm_mlp[i], w_up, w_down, i, norm_final, last)
        xs = _mlp(xs, norm_mlp[i], w_up, w_down, i, norm_final, last)
    return (xp.reshape(bsz, seq, d), xs.reshape(dbsz, dseq, d),
            jnp.stack(kp_l), jnp.stack(vp_l), jnp.stack(cp_l), jnp.stack(hp_l),
            jnp.stack(ks_l), jnp.stack(vs_l), jnp.stack(cs_l), jnp.stack(hs_l))
```

```python
import functools
import math

import jax
import jax.numpy as jnp
from jax import lax
from jax.experimental import pallas as pl
from jax.experimental.pallas import tpu as pltpu

F32 = jnp.float32
BF16 = jnp.bfloat16

D_MODEL = 2048
DEPTH = 4
PAGE_SIZE = 128
D_RNN = D_MODEL
RG_BLOCK = 256
RG_BLOCKS = D_RNN // RG_BLOCK
CONV_W = 4
LRU_C = 8.0
QK_DIM = 128
V_DIM = 2 * QK_DIM
N_HEADS = D_MODEL // V_DIM
QK_W = N_HEADS * 2 * QK_DIM
V_W = N_HEADS * V_DIM
D_FF = 4 * D_MODEL
EPS = 1e-6
QK_SCALE = QK_DIM ** -0.5

VMEM_LIMIT_BYTES = 56 * 1024 * 1024
SUBLANES = 8
LANES = 128
LOG2_E = math.log2(math.e)

ROW_TILE = 1024
RES_ROW_TILE = 512
COL_TILE = 1024
GROUP_COL_TILE = 512
FF_TILE = 512
Q_TILE = 512
SCAN_TILE = 256
PAGES_PER_STEP = 4
CONV_PAD = SUBLANES

K_ROWS = PAGE_SIZE * N_HEADS * 2
V_ROWS = PAGE_SIZE * N_HEADS
NEW_TOKENS_PAD = LANES // N_HEADS


def _params(*semantics):
    return pltpu.CompilerParams(dimension_semantics=semantics,
                                vmem_limit_bytes=VMEM_LIMIT_BYTES)


def _rms(x, g):
    return x * lax.rsqrt(jnp.mean(x * x, axis=-1, keepdims=True) + EPS) * g


def _lambda_init(layer_idx):
    return 0.8 - 0.6 * math.exp(-0.3 * layer_idx)


def _norm_proj_kernel(x_ref, g_ref, *refs, n_out, scales):
    w_refs, o_refs, xn_ref = refs[:n_out], refs[n_out:2 * n_out], refs[2 * n_out]

    @pl.when(pl.program_id(1) == 0)
    def _():
        xn_ref[...] = _rms(x_ref[...], g_ref[...]).astype(BF16)

    xn = xn_ref[...]
    for w_ref, o_ref, scale in zip(w_refs, o_refs, scales):
        acc = jnp.dot(xn, w_ref[...], preferred_element_type=F32)
        if scale != 1.0:
            acc = acc * scale
        o_ref[...] = acc.astype(o_ref.dtype)


def _norm_proj(x, g, w, layer, group_width, out_dtypes, scales):
    m, d = x.shape
    n_out = len(out_dtypes)
    assert w.shape[1:] == (d, n_out * group_width)
    tm = min(m, ROW_TILE)
    tn = COL_TILE if n_out == 1 else GROUP_COL_TILE
    nj = group_width // tn
    in_specs = [pl.BlockSpec((tm, d), lambda i, j: (i, 0)),
                pl.BlockSpec((1, d), lambda i, j: (0, 0))]
    for k in range(n_out):
        in_specs.append(pl.BlockSpec((None, d, tn), lambda i, j, k=k: (layer, 0, k * nj + j)))
    return pl.pallas_call(
        functools.partial(_norm_proj_kernel, n_out=n_out, scales=tuple(scales)),
        grid=(m // tm, nj),
        in_specs=in_specs,
        out_specs=[pl.BlockSpec((tm, tn), lambda i, j: (i, j))] * n_out,
        out_shape=[jax.ShapeDtypeStruct((m, group_width), dt) for dt in out_dtypes],
        scratch_shapes=[pltpu.VMEM((tm, d), BF16)],
        compiler_params=_params("parallel", "arbitrary"),
        name="norm_proj",
    )(x, g.reshape(1, d), *([w] * n_out))


def _proj_res_kernel(a_ref, w_ref, r_ref, o_ref):
    o_ref[...] = r_ref[...] + jnp.dot(a_ref[...].astype(BF16), w_ref[...],
                                      preferred_element_type=F32)


def _proj_res(a, w, layer, res):
    m, k = a.shape
    n = w.shape[2]
    tm = min(m, RES_ROW_TILE)
    return pl.pallas_call(
        _proj_res_kernel,
        grid=(m // tm,),
        in_specs=[pl.BlockSpec((tm, k), lambda i: (i, 0)),
                  pl.BlockSpec((None, k, n), lambda i: (layer, 0, 0)),
                  pl.BlockSpec((tm, n), lambda i: (i, 0))],
        out_specs=pl.BlockSpec((tm, n), lambda i: (i, 0)),
        out_shape=jax.ShapeDtypeStruct((m, n), F32),
        compiler_params=_params("parallel"),
        name="proj_res",
    )(a, w, res)


def _mlp_step(xn_ref, wu_ref, wd_ref, gf_ref, o_ref, final_norm):
    h = jnp.dot(xn_ref[...], wu_ref[...], preferred_element_type=F32)
    h = jnp.square(jnp.maximum(h, 0.0)).astype(BF16)
    o_ref[...] += jnp.dot(h, wd_ref[...], preferred_element_type=F32)

    if final_norm:
        @pl.when(pl.program_id(1) == pl.num_programs(1) - 1)
        def _():
            o_ref[...] = _rms(o_ref[...], gf_ref[...])


def _mlp_kernel(x_ref, g_ref, wu_ref, wd_ref, gf_ref, o_ref, xn_ref, *, final_norm):
    @pl.when(pl.program_id(1) == 0)
    def _():
        x = x_ref[...]
        xn_ref[...] = _rms(x, g_ref[...]).astype(BF16)
        o_ref[...] = x

    _mlp_step(xn_ref, wu_ref, wd_ref, gf_ref, o_ref, final_norm)


def _mlp(x, g, w_up, w_down, layer, g_final, final_norm):
    m, d = x.shape
    ff = w_up.shape[2]
    tm = min(m, ROW_TILE)
    tf = FF_TILE
    return pl.pallas_call(
        functools.partial(_mlp_kernel, final_norm=final_norm),
        grid=(m // tm, ff // tf),
        in_specs=[pl.BlockSpec((tm, d), lambda i, f: (i, 0)),
                  pl.BlockSpec((1, d), lambda i, f: (0, 0)),
                  pl.BlockSpec((None, d, tf), lambda i, f: (layer, 0, f)),
                  pl.BlockSpec((None, tf, d), lambda i, f: (layer, f, 0)),
                  pl.BlockSpec((1, d), lambda i, f: (0, 0))],
        out_specs=pl.BlockSpec((tm, d), lambda i, f: (i, 0)),
        out_shape=jax.ShapeDtypeStruct((m, d), F32),
        scratch_shapes=[pltpu.VMEM((tm, d), BF16)],
        compiler_params=_params("parallel", "arbitrary"),
        name="mlp",
    )(x, g.reshape(1, d), w_up, w_down, g_final.reshape(1, d))


def _sigmoid(x):
    return 1.0 / (1.0 + jnp.exp(-x))


def _gelu_tanh(x):
    c = math.sqrt(2.0 / math.pi)
    return x * (0.5 * (1.0 + jnp.tanh(c * (x + 0.044715 * (x * x * x)))))


def _rg_kernel(xb_ref, gb_ref, cbuf_ref, h0_ref, cw_ref, cb_ref, wa_ref, ba_ref, wi_ref, bi_ref,
               lam_ref, y_ref, cout_ref, hout_ref, win_ref, *, seq, tt):
    win_ref[0:CONV_PAD, :] = jnp.zeros((CONV_PAD, RG_BLOCK), F32)
    win_ref[CONV_PAD - (CONV_W - 1):CONV_PAD, :] = cbuf_ref[...]
    win_ref[CONV_PAD:CONV_PAD + seq, :] = xb_ref[...]
    cout_ref[...] = win_ref[CONV_PAD + seq - (CONV_W - 1):CONV_PAD + seq, :]

    cw = cw_ref[...]
    cb = cb_ref[...]
    wa = wa_ref[...]
    wi = wi_ref[...]
    ba = ba_ref[...]
    bi = bi_ref[...]
    neg_lam = -lam_ref[...]
    log_a_scale = -LRU_C * (jnp.maximum(neg_lam, 0.0) + jnp.log(1.0 + jnp.exp(-jnp.abs(neg_lam))))

    def chunk(base, h):
        w = win_ref[pl.ds(base, tt + CONV_PAD), :]
        u = cb
        for k in range(CONV_W):
            off = CONV_PAD - (CONV_W - 1) + k
            u = u + w[off:off + tt] * cw[k:k + 1]
        ub = u.astype(BF16)
        r = _sigmoid(jnp.dot(ub, wa, preferred_element_type=F32) + ba)
        i = _sigmoid(jnp.dot(ub, wi, preferred_element_type=F32) + bi)
        a = jnp.exp(log_a_scale * r)
        b = jnp.sqrt(1.0 - a * a) * (i * u)
        if tt <= SUBLANES:
            rows = []
            for t in range(tt):
                h = a[t:t + 1] * h + b[t:t + 1]
                rows.append(h)
            hseq = jnp.concatenate(rows, axis=0)
        else:
            sub = lax.broadcasted_iota(jnp.int32, (SUBLANES, RG_BLOCK), 0)
            rows = []
            for g in range(tt // SUBLANES):
                ag = a[g * SUBLANES:(g + 1) * SUBLANES]
                bg = b[g * SUBLANES:(g + 1) * SUBLANES]
                d = 1
                while d < SUBLANES:
                    keep = sub >= d
                    a_sh = jnp.where(keep, pltpu.roll(ag, d, 0), 1.0)
                    b_sh = jnp.where(keep, pltpu.roll(bg, d, 0), 0.0)
                    bg = ag * b_sh + bg
                    ag = ag * a_sh
                    d *= 2
                hg = ag * h + bg
                h = hg[SUBLANES - 1:SUBLANES]
                rows.append(hg)
            hseq = jnp.concatenate(rows, axis=0)
        gate = _gelu_tanh(gb_ref[pl.ds(base, tt), :])
        y_ref[pl.ds(base, tt), :] = (hseq * gate).astype(y_ref.dtype)
        return h

    h0 = h0_ref[...]
    if seq == tt:
        h_last = chunk(0, h0)
    else:
        h_last = lax.fori_loop(
            0, seq // tt, lambda c, h: chunk(pl.multiple_of(c * tt, tt), h), h0)
    hout_ref[...] = h_last


def _rg_mix(xz, conv_buf, h0, conv_w, conv_b, w_a, b_a, w_i, b_i, lam, y_dtype):
    bsz, seq, _ = xz.shape
    tt = min(seq, SCAN_TILE)
    nb = RG_BLOCKS
    vec = lambda: pl.BlockSpec((1, RG_BLOCK), lambda b, j: (0, j))
    win_rows = -(-(seq + CONV_PAD) // SUBLANES) * SUBLANES
    return pl.pallas_call(
        functools.partial(_rg_kernel, seq=seq, tt=tt),
        grid=(bsz, nb),
        in_specs=[pl.BlockSpec((None, seq, RG_BLOCK), lambda b, j: (b, 0, j)),
                  pl.BlockSpec((None, seq, RG_BLOCK), lambda b, j: (b, 0, nb + j)),
                  pl.BlockSpec((None, CONV_W - 1, RG_BLOCK), lambda b, j: (b, 0, j)),
                  pl.BlockSpec((None, 1, RG_BLOCK), lambda b, j: (b, 0, j)),
                  pl.BlockSpec((CONV_W, RG_BLOCK), lambda b, j: (0, j)),
                  vec(),
                  pl.BlockSpec((None, RG_BLOCK, RG_BLOCK), lambda b, j: (j, 0, 0)),
                  vec(),
                  pl.BlockSpec((None, RG_BLOCK, RG_BLOCK), lambda b, j: (j, 0, 0)),
                  vec(),
                  vec()],
        out_specs=[pl.BlockSpec((None, seq, RG_BLOCK), lambda b, j: (b, 0, j)),
                   pl.BlockSpec((None, CONV_W - 1, RG_BLOCK), lambda b, j: (b, 0, j)),
                   pl.BlockSpec((None, 1, RG_BLOCK), lambda b, j: (b, 0, j))],
        out_shape=[jax.ShapeDtypeStruct((bsz, seq, D_RNN), y_dtype),
                   jax.ShapeDtypeStruct((bsz, CONV_W - 1, D_RNN), F32),
                   jax.ShapeDtypeStruct((bsz, 1, D_RNN), F32)],
        scratch_shapes=[pltpu.VMEM((win_rows, RG_BLOCK), F32)],
        compiler_params=_params("parallel", "parallel"),
        name="rg_mix",
    )(xz, xz, conv_buf, h0.reshape(bsz, 1, D_RNN), conv_w, conv_b.reshape(1, D_RNN),
      w_a, b_a.reshape(1, D_RNN), w_i, b_i.reshape(1, D_RNN), lam.reshape(1, D_RNN))


def _diff_lambda(lq1_ref, lk1_ref, lq2_ref, lk2_ref, lam_init):
    s1 = jnp.sum(lq1_ref[...] * lk1_ref[...], axis=-1, keepdims=True)
    s2 = jnp.sum(lq2_ref[...] * lk2_ref[...], axis=-1, keepdims=True)
    return jnp.exp(s1) - jnp.exp(s2) + lam_init


def _head_norm(o, subln, lam_init):
    on = o * lax.rsqrt(jnp.mean(o * o, axis=-1, keepdims=True) + EPS)
    return on * subln * (1.0 - lam_init)


_NT = (((1,), (1,)), ((), ()))


def _lane_tiles(x, n):
    return x if n == 1 else jnp.concatenate([x] * n, axis=1)


def _softmax_update(s, v_blocks, m_ref, l_ref, acc_ref):
    n = s.shape[1] // LANES
    m_prev = m_ref[...]
    m_new = jnp.maximum(m_prev, jnp.max(s, axis=-1, keepdims=True))
    alpha = jnp.exp2(m_prev - m_new)
    p = jnp.exp2(s - _lane_tiles(m_new, n))
    p_lanes = p[:, :LANES]
    for c in range(1, n):
        p_lanes = p_lanes + p[:, c * LANES:(c + 1) * LANES]
    l_ref[...] = alpha * l_ref[...] + p_lanes
    pb = p.astype(BF16)
    pv = None
    for off, cols, vb in v_blocks:
        part = jnp.dot(pb[:, off:off + cols], vb, preferred_element_type=F32)
        pv = part if pv is None else pv + part
    acc_ref[...] = _lane_tiles(alpha, acc_ref.shape[1] // LANES) * acc_ref[...] + pv
    m_ref[...] = m_new


def _prompt_attn_kernel(q_ref, k_ref, v_ref, lq1_ref, lk1_ref, lq2_ref, lk2_ref, subln_ref,
                        o_ref, kb_ref, vb_ref, m_ref, l_ref, acc_ref, *, tq, lam_init):
    qi = pl.program_id(2)

    @pl.when(qi == 0)
    def _():
        kb_ref[...] = k_ref[...].astype(BF16)
        vb_ref[...] = v_ref[...].astype(BF16)

    m_ref[...] = jnp.full(m_ref.shape, -jnp.inf, F32)
    l_ref[...] = jnp.zeros(l_ref.shape, F32)
    acc_ref[...] = jnp.zeros(acc_ref.shape, F32)

    q = q_ref[...]
    q1, q2 = q[:, :QK_DIM], q[:, QK_DIM:]

    def block(j, masked):
        start = pl.multiple_of(j * tq, tq)
        kj = kb_ref[pl.ds(start, tq), :]
        vj = vb_ref[pl.ds(start, tq), :]
        s1 = lax.dot_general(q1, kj[:, :QK_DIM], _NT, preferred_element_type=F32)
        s2 = lax.dot_general(q2, kj[:, QK_DIM:], _NT, preferred_element_type=F32)
        s = jnp.concatenate([s1, s2], axis=0)
        if masked:
            row = lax.broadcasted_iota(jnp.int32, (tq, tq), 0)
            col = lax.broadcasted_iota(jnp.int32, (tq, tq), 1)
            keep = col <= row
            s = jnp.where(jnp.concatenate([keep, keep], axis=0), s, -jnp.inf)
        _softmax_update(s, [(0, tq, vj)], m_ref, l_ref, acc_ref)

    def body(j, carry):
        block(j, False)
        return carry

    lax.fori_loop(0, qi, body, 0)
    block(qi, True)

    lam = _diff_lambda(lq1_ref, lk1_ref, lq2_ref, lk2_ref, lam_init)
    pn = acc_ref[...] / jnp.sum(l_ref[...], axis=-1, keepdims=True)
    o = pn[:tq] - lam * pn[tq:]
    o_ref[...] = _head_norm(o, subln_ref[...], lam_init).astype(o_ref.dtype)


def _prompt_attn(q, k, v, lq1, lk1, lq2, lk2, subln, lam_init, bsz, seq):
    m = bsz * seq
    tq = Q_TILE
    nq = seq // tq
    lvec = lambda: pl.BlockSpec((1, QK_DIM), lambda b, h, i: (0, 0))
    return pl.pallas_call(
        functools.partial(_prompt_attn_kernel, tq=tq, lam_init=lam_init),
        grid=(bsz, N_HEADS, nq),
        in_specs=[pl.BlockSpec((tq, V_DIM), lambda b, h, i: (b * nq + i, h)),
                  pl.BlockSpec((seq, V_DIM), lambda b, h, i: (b, h)),
                  pl.BlockSpec((seq, V_DIM), lambda b, h, i: (b, h)),
                  lvec(), lvec(), lvec(), lvec(),
                  pl.BlockSpec((1, V_DIM), lambda b, h, i: (0, 0))],
        out_specs=pl.BlockSpec((tq, V_DIM), lambda b, h, i: (b * nq + i, h)),
        out_shape=jax.ShapeDtypeStruct((m, V_W), BF16),
        scratch_shapes=[pltpu.VMEM((seq, V_DIM), BF16),
                        pltpu.VMEM((seq, V_DIM), BF16),
                        pltpu.VMEM((2 * tq, LANES), F32),
                        pltpu.VMEM((2 * tq, LANES), F32),
                        pltpu.VMEM((2 * tq, V_DIM), F32)],
        compiler_params=_params("parallel", "parallel", "arbitrary"),
        name="prompt_attn",
    )(q, k, v, lq1.reshape(1, -1), lk1.reshape(1, -1), lq2.reshape(1, -1), lk2.reshape(1, -1),
      subln.reshape(1, -1))


def _decode_scores(qb_ref, bias, k_page, n_cols):
    per_map = []
    for mp in range(2):
        km = k_page[pl.ds(mp, n_cols, stride=2), :].astype(BF16)
        per_map.append(lax.dot_general(qb_ref[mp], km, _NT, preferred_element_type=F32))
    return jnp.concatenate(per_map, axis=0) + bias


def _mlp_attn_kernel(pt_ref, x_hbm, g_ref, wu_ref, wd_ref, gf_ref,
                     q_ref, m_in, l_in, acc_in, knew_ref, vnew_ref,
                     lq1_ref, lk1_ref, lq2_ref, lk2_ref, subln_ref, *refs,
                     n_pages, dec_seq, lam_init, tm, final_norm, first_half):
    del pt_ref
    k_refs, v_refs = refs[:n_pages], refs[n_pages:2 * n_pages]
    rest = refs[2 * n_pages:]
    if first_half:
        o_ref, m_out, l_out, acc_out = rest[:4]
        rest = rest[4:]
    else:
        o_ref, heads_ref = rest[:2]
        rest = rest[2:]
    xn_ref, x_sem, qb_ref, bias_ref, m_ref, l_ref, acc_ref = rest
    i = pl.program_id(0)
    f = pl.program_id(1)
    n_q = N_HEADS * dec_seq

    @pl.when(f == 0)
    def _():
        cp = pltpu.make_async_copy(x_hbm.at[pl.ds(pl.multiple_of(i * tm, tm), tm), :], o_ref, x_sem)
        cp.start()
        qb_ref[...] = (q_ref[...] * (QK_SCALE * LOG2_E)).astype(BF16)
        row = lax.broadcasted_iota(jnp.int32, bias_ref.shape, 0)
        col = lax.broadcasted_iota(jnp.int32, bias_ref.shape, 1)
        same_head = (row % n_q) // dec_seq == col % N_HEADS
        bias_ref[...] = jnp.where(same_head, 0.0, -jnp.inf)
        if first_half:
            m_ref[...] = jnp.full(m_ref.shape, -jnp.inf, F32)
            l_ref[...] = jnp.zeros(l_ref.shape, F32)
            acc_ref[...] = jnp.zeros(acc_ref.shape, F32)
        else:
            m_ref[...] = m_in[...]
            l_ref[...] = l_in[...]
            acc_ref[...] = acc_in[...]
        cp.wait()
        xn_ref[...] = _rms(o_ref[...], g_ref[...]).astype(BF16)

    bias = bias_ref[...]
    s = jnp.concatenate([_decode_scores(qb_ref, bias, kp, V_ROWS) for kp in k_refs], axis=1)
    v_blocks = [(p * V_ROWS, V_ROWS, vp[...].astype(BF16)) for p, vp in enumerate(v_refs)]
    _softmax_update(s, v_blocks, m_ref, l_ref, acc_ref)

    _mlp_step(xn_ref, wu_ref, wd_ref, gf_ref, o_ref, final_norm)

    @pl.when(f == pl.num_programs(1) - 1)
    def _():
        if first_half:
            m_out[...] = m_ref[...]
            l_out[...] = l_ref[...]
            acc_out[...] = acc_ref[...]
        else:
            s_new = _decode_scores(qb_ref, bias_ref[:, :LANES], knew_ref, LANES)
            t = lax.broadcasted_iota(jnp.int32, s_new.shape, 0) % dec_seq
            tok = lax.broadcasted_iota(jnp.int32, s_new.shape, 1) // N_HEADS
            s_new = jnp.where(tok <= t, s_new, -jnp.inf)
            _softmax_update(s_new, [(0, LANES, vnew_ref[...].astype(BF16))], m_ref, l_ref, acc_ref)
            lam = _diff_lambda(lq1_ref, lk1_ref, lq2_ref, lk2_ref, lam_init)
            pn = acc_ref[...] / jnp.sum(l_ref[...], axis=-1, keepdims=True)
            o = pn[:n_q] - lam * pn[n_q:]
            heads_ref[...] = _head_norm(o, subln_ref[...], lam_init)


def _mlp_attn(x, g, w_up, w_down, layer, g_final, final_norm,
              q_arr, state, k_page, v_page, cache_k, cache_v, cache_layer, page_table,
              lq1, lk1, lq2, lk2, subln, lam_init, dec_seq):
    m, d = x.shape
    ff = w_up.shape[2]
    tm, tf, gp = ROW_TILE, FF_TILE, PAGES_PER_STEP
    bsz = q_arr.shape[0]
    n_q = N_HEADS * dec_seq
    n_steps = ff // tf
    first_half = state is None
    assert m // tm == bsz and 2 * n_steps * gp == page_table.shape[1]
    page0 = 0 if first_half else n_steps * gp

    def page_spec(p, rows, width):
        return pl.BlockSpec((None, None, rows, width),
                            lambda i, f, pt, p=p: (cache_layer, pt[i, page0 + f * gp + p], 0, 0))

    const = lambda shape: pl.BlockSpec(shape, lambda i, f, pt: (0,) * len(shape))
    per_seq = lambda rows, width: pl.BlockSpec((None, rows, width), lambda i, f, pt: (i, 0, 0))
    state_specs = [per_seq(2 * n_q, LANES), per_seq(2 * n_q, LANES), per_seq(2 * n_q, V_DIM)]
    state_shapes = [jax.ShapeDtypeStruct((bsz, 2 * n_q, LANES), F32),
                    jax.ShapeDtypeStruct((bsz, 2 * n_q, LANES), F32),
                    jax.ShapeDtypeStruct((bsz, 2 * n_q, V_DIM), F32)]
    if first_half:
        state = [jnp.zeros(s.shape, F32) for s in state_shapes]
        extra_specs, extra_shapes = state_specs, state_shapes
    else:
        extra_specs = [per_seq(n_q, V_DIM)]
        extra_shapes = [jax.ShapeDtypeStruct((bsz, n_q, V_DIM), F32)]
    grid_spec = pltpu.PrefetchScalarGridSpec(
        num_scalar_prefetch=1,
        grid=(m // tm, n_steps),
        in_specs=[pl.BlockSpec(memory_space=pl.ANY),
                  const((1, d)),
                  pl.BlockSpec((None, d, tf), lambda i, f, pt: (layer, 0, f)),
                  pl.BlockSpec((None, tf, d), lambda i, f, pt: (layer, f, 0)),
                  const((1, d)),
                  pl.BlockSpec((None, 2, n_q, QK_DIM), lambda i, f, pt: (i, 0, 0, 0))]
                 + state_specs
                 + [per_seq(2 * LANES, QK_DIM), per_seq(LANES, V_DIM),
                    const((1, QK_DIM)), const((1, QK_DIM)), const((1, QK_DIM)), const((1, QK_DIM)),
                    const((1, V_DIM))]
                 + [page_spec(p, K_ROWS, QK_DIM) for p in range(gp)]
                 + [page_spec(p, V_ROWS, V_DIM) for p in range(gp)],
        out_specs=[pl.BlockSpec((tm, d), lambda i, f, pt: (i, 0))] + extra_specs,
        scratch_shapes=[pltpu.VMEM((tm, d), BF16),
                        pltpu.SemaphoreType.DMA(()),
                        pltpu.VMEM((2, n_q, QK_DIM), BF16),
                        pltpu.VMEM((2 * n_q, V_ROWS), F32),
                        pltpu.VMEM((2 * n_q, LANES), F32),
                        pltpu.VMEM((2 * n_q, LANES), F32),
                        pltpu.VMEM((2 * n_q, V_DIM), F32)],
    )
    return pl.pallas_call(
        functools.partial(_mlp_attn_kernel, n_pages=gp, dec_seq=dec_seq, lam_init=lam_init, tm=tm,
                          final_norm=final_norm, first_half=first_half),
        grid_spec=grid_spec,
        out_shape=[jax.ShapeDtypeStruct((m, d), F32)] + extra_shapes,
        compiler_params=_params("parallel", "arbitrary"),
        name="mlp_attn",
    )(page_table, x, g.reshape(1, d), w_up, w_down, g_final.reshape(1, d), q_arr, *state,
      k_page, v_page, lq1.reshape(1, -1), lk1.reshape(1, -1), lq2.reshape(1, -1),
      lk2.reshape(1, -1), subln.reshape(1, -1), *([cache_k] * gp), *([cache_v] * gp))


def _decode_operands(q, k_new, v_new):
    bsz, dec_seq, _ = q.shape
    q_arr = q.reshape(bsz, dec_seq, N_HEADS, 2, QK_DIM).transpose(0, 3, 2, 1, 4)
    q_arr = q_arr.reshape(bsz, 2, N_HEADS * dec_seq, QK_DIM)
    pad_tokens = NEW_TOKENS_PAD - dec_seq
    k_page = jnp.pad(k_new.reshape(bsz, dec_seq * N_HEADS * 2, QK_DIM),
                     ((0, 0), (0, pad_tokens * N_HEADS * 2), (0, 0)))
    v_page = jnp.pad(v_new.reshape(bsz, dec_seq * N_HEADS, V_DIM),
                     ((0, 0), (0, pad_tokens * N_HEADS), (0, 0)))
    return q_arr, k_page, v_page


def kernel(x_prompt, x_sample, cache_k, cache_v, state_conv, state_h, page_table, norm_mix, norm_mlp, norm_final, rg_w_in, rg_conv_w, rg_conv_b, rg_w_a, rg_b_a, rg_w_i, rg_b_i, rg_lambda, rg_w_out, da_w_qkv, da_lq1, da_lk1, da_lq2, da_lk2, da_subln, da_w_o, mlp_w_up, mlp_w_down):
    bsz, seq, d = x_prompt.shape
    dbsz, dseq, _ = x_sample.shape
    xp = x_prompt.reshape(bsz * seq, d)
    xs = x_sample.reshape(dbsz * dseq, d)
    n_layers_b, n_pool = cache_k.shape[:2]
    ck = cache_k.reshape(n_layers_b, n_pool, K_ROWS, QK_DIM)
    cv = cache_v.reshape(n_layers_b, n_pool, V_ROWS, V_DIM)

    w_in, w_out = rg_w_in.astype(BF16), rg_w_out.astype(BF16)
    w_a, w_i = rg_w_a.astype(BF16), rg_w_i.astype(BF16)
    w_qkv, w_o = da_w_qkv.astype(BF16), da_w_o.astype(BF16)
    w_up, w_down = mlp_w_up.astype(BF16), mlp_w_down.astype(BF16)

    kp_l, vp_l, cp_l, hp_l = [], [], [], []
    ks_l, vs_l, cs_l, hs_l = [], [], [], []

    def rg_params(j):
        return (rg_conv_w[j], rg_conv_b[j], w_a[j], rg_b_a[j], w_i[j], rg_b_i[j], rg_lambda[j])

    def sample_rg_layer(xs, i):
        j = i // 2
        (xz,) = _norm_proj(xs, norm_mix[i], w_in, j, 2 * D_RNN, [F32], [1.0])
        ys, cbs, hls = _rg_mix(xz.reshape(dbsz, dseq, 2 * D_RNN), state_conv[j], state_h[j],
                               *rg_params(j), F32)
        cs_l.append(cbs)
        hs_l.append(hls.reshape(dbsz, D_RNN))
        xs = _proj_res(ys.reshape(dbsz * dseq, D_RNN), w_out, j, xs)
        return _mlp(xs, norm_mlp[i], w_up, w_down, i, norm_final, False)

    def sample_qkv(xs, i):
        j = i // 2
        qs, kn, vn = _norm_proj(xs, norm_mix[i], w_qkv, j, QK_W, [F32, F32, F32], [1.0, 1.0, 1.0])
        ks_l.append(kn.reshape(dbsz, dseq, N_HEADS, 2, QK_DIM))
        vs_l.append(vn.reshape(dbsz, dseq, N_HEADS, V_DIM))
        return _decode_operands(qs.reshape(dbsz, dseq, QK_W), kn.reshape(dbsz, dseq, QK_W),
                                vn.reshape(dbsz, dseq, V_W))

    def sample_attn_out(xs, heads, i, last):
        j = i // 2
        os_ = heads.reshape(dbsz, N_HEADS, dseq, V_DIM).transpose(0, 2, 1, 3).reshape(dbsz * dseq, V_W)
        xs = _proj_res(os_, w_o, j, xs)
        return _mlp(xs, norm_mlp[i], w_up, w_down, i, norm_final, last)

    for i in range(DEPTH):
        j = i // 2
        if i % 2 == 0:
            xs = sample_rg_layer(xs, i)
            dec = sample_qkv(xs, i + 1)
            dj = (i + 1) // 2
            dec_lam_init = _lambda_init(i + 1)
            dec_vecs = (da_lq1[dj], da_lk1[dj], da_lq2[dj], da_lk2[dj], da_subln[dj])

            (xz_p,) = _norm_proj(xp, norm_mix[i], w_in, j, 2 * D_RNN, [F32], [1.0])
            yp, cbp, hlp = _rg_mix(xz_p.reshape(bsz, seq, 2 * D_RNN),
                                   jnp.zeros((bsz, CONV_W - 1, D_RNN), F32),
                                   jnp.zeros((bsz, D_RNN), F32), *rg_params(j), BF16)
            cp_l.append(cbp)
            hp_l.append(hlp.reshape(bsz, D_RNN))
            xp = _proj_res(yp.reshape(bsz * seq, D_RNN), w_out, j, xp)
            xp, *dec_state = _mlp_attn(xp, norm_mlp[i], w_up, w_down, i, norm_final, False,
                                       dec[0], None, dec[1], dec[2], ck, cv, dj, page_table,
                                       *dec_vecs, dec_lam_init, dseq)
        else:
            lam_init = _lambda_init(i)
            lvecs = (da_lq1[j], da_lk1[j], da_lq2[j], da_lk2[j], da_subln[j])
            qp, kp, vp = _norm_proj(xp, norm_mix[i], w_qkv, j, QK_W, [BF16, F32, F32],
                                    [QK_SCALE * LOG2_E, 1.0, 1.0])
            op = _prompt_attn(qp, kp, vp, *lvecs, lam_init, bsz, seq)
            kp_l.append(kp.reshape(bsz, seq, N_HEADS, 2, QK_DIM))
            vp_l.append(vp.reshape(bsz, seq, N_HEADS, V_DIM))
            xp = _proj_res(op, w_o, j, xp)
            last = i == DEPTH - 1
            xp, heads = _mlp_attn(xp, norm_mlp[i], w_up, w_down, i, norm_final, last,
                                  dec[0], dec_state, dec[1], dec[2], ck, cv, j, page_table,
                                  *lvecs, lam_init, dseq)
            xs = sample_attn_out(xs, heads, i, last)
    return (xp.reshape(bsz, seq, d), xs.reshape(dbsz, dseq, d),
            jnp.stack(kp_l), jnp.stack(vp_l), jnp.stack(cp_l), jnp.stack(hp_l),
            jnp.stack(ks_l), jnp.stack(vs_l), jnp.stack(cs_l), jnp.stack(hs_l))
```

```python
import functools
import math

import jax
import jax.numpy as jnp
from jax import lax
from jax.experimental import pallas as pl
from jax.experimental.pallas import tpu as pltpu

F32 = jnp.float32
BF16 = jnp.bfloat16

D_MODEL = 2048
DEPTH = 4
PAGE_SIZE = 128
D_RNN = D_MODEL
RG_BLOCK = 256
RG_BLOCKS = D_RNN // RG_BLOCK
CONV_W = 4
LRU_C = 8.0
QK_DIM = 128
V_DIM = 2 * QK_DIM
N_HEADS = D_MODEL // V_DIM
QK_W = N_HEADS * 2 * QK_DIM
V_W = N_HEADS * V_DIM
D_FF = 4 * D_MODEL
EPS = 1e-6
QK_SCALE = QK_DIM ** -0.5

VMEM_LIMIT_BYTES = 56 * 1024 * 1024
SUBLANES = 8
LANES = 128
LOG2_E = math.log2(math.e)

ROW_TILE = 1024
RES_ROW_TILE = 512
COL_TILE = 1024
GROUP_COL_TILE = 512
FF_TILE = 512
Q_TILE = 512
SCAN_TILE = 256
PAGES_PER_STEP = 4
CONV_PAD = SUBLANES

KV_GROUPS = N_HEADS * 2
K_ROWS = PAGE_SIZE * KV_GROUPS
V_ROWS = PAGE_SIZE * N_HEADS
NEW_TOKENS_PAD = LANES // N_HEADS


def _params(*semantics):
    return pltpu.CompilerParams(dimension_semantics=semantics,
                                vmem_limit_bytes=VMEM_LIMIT_BYTES)


def _rms(x, g):
    return x * lax.rsqrt(jnp.mean(x * x, axis=-1, keepdims=True) + EPS) * g


def _lambda_init(layer_idx):
    return 0.8 - 0.6 * math.exp(-0.3 * layer_idx)


def _norm_proj_kernel(x_ref, g_ref, *refs, n_out, scales):
    w_refs, o_refs, xn_ref = refs[:n_out], refs[n_out:2 * n_out], refs[2 * n_out]

    @pl.when(pl.program_id(1) == 0)
    def _():
        xn_ref[...] = _rms(x_ref[...], g_ref[...]).astype(BF16)

    xn = xn_ref[...]
    for w_ref, o_ref, scale in zip(w_refs, o_refs, scales):
        acc = jnp.dot(xn, w_ref[...], preferred_element_type=F32)
        if scale != 1.0:
            acc = acc * scale
        o_ref[...] = acc.astype(o_ref.dtype)


def _norm_proj(x, g, w, layer, group_width, out_dtypes, scales):
    m, d = x.shape
    n_out = len(out_dtypes)
    assert w.shape[1:] == (d, n_out * group_width)
    tm = min(m, ROW_TILE)
    tn = COL_TILE if n_out == 1 else GROUP_COL_TILE
    nj = group_width // tn
    in_specs = [pl.BlockSpec((tm, d), lambda i, j: (i, 0)),
                pl.BlockSpec((1, d), lambda i, j: (0, 0))]
    for k in range(n_out):
        in_specs.append(pl.BlockSpec((None, d, tn), lambda i, j, k=k: (layer, 0, k * nj + j)))
    return pl.pallas_call(
        functools.partial(_norm_proj_kernel, n_out=n_out, scales=tuple(scales)),
        grid=(m // tm, nj),
        in_specs=in_specs,
        out_specs=[pl.BlockSpec((tm, tn), lambda i, j: (i, j))] * n_out,
        out_shape=[jax.ShapeDtypeStruct((m, group_width), dt) for dt in out_dtypes],
        scratch_shapes=[pltpu.VMEM((tm, d), BF16)],
        compiler_params=_params("parallel", "arbitrary"),
        name="norm_proj",
    )(x, g.reshape(1, d), *([w] * n_out))


def _qkv_proj_kernel(x_ref, g_ref, wq_ref, wk_ref, wv_ref, q_ref, kb_ref, vb_ref, kf_ref, v_ref,
                     xn_ref, *, q_scale):
    j = pl.program_id(1)

    @pl.when(j == 0)
    def _():
        xn_ref[...] = _rms(x_ref[...], g_ref[...]).astype(BF16)

    xn = xn_ref[...]
    q = jnp.dot(xn, wq_ref[...], preferred_element_type=F32)
    q_ref[...] = (q * q_scale).astype(BF16)
    k = jnp.dot(xn, wk_ref[...], preferred_element_type=F32)
    kb_ref[...] = k.astype(BF16)
    v = jnp.dot(xn, wv_ref[...], preferred_element_type=F32)
    vb_ref[...] = v.astype(BF16)
    v_ref[...] = v
    tm, tn = k.shape
    for c in range(tn // QK_DIM):
        kf_ref[pl.ds(j * (tn // QK_DIM) + c, tm, stride=KV_GROUPS), :] = k[:, c * QK_DIM:(c + 1) * QK_DIM]


def _qkv_proj(x, g, w, layer, q_scale):
    m, d = x.shape
    tm, tn = RES_ROW_TILE, GROUP_COL_TILE
    nj = QK_W // tn
    w_spec = lambda k: pl.BlockSpec((None, d, tn), lambda i, j, k=k: (layer, 0, k * nj + j))
    tile = pl.BlockSpec((tm, tn), lambda i, j: (i, j))
    return pl.pallas_call(
        functools.partial(_qkv_proj_kernel, q_scale=q_scale),
        grid=(m // tm, nj),
        in_specs=[pl.BlockSpec((tm, d), lambda i, j: (i, 0)),
                  pl.BlockSpec((1, d), lambda i, j: (0, 0)),
                  w_spec(0), w_spec(1), w_spec(2)],
        out_specs=[tile, tile, tile,
                   pl.BlockSpec((tm * KV_GROUPS, QK_DIM), lambda i, j: (i, 0)),
                   tile],
        out_shape=[jax.ShapeDtypeStruct((m, QK_W), BF16),
                   jax.ShapeDtypeStruct((m, QK_W), BF16),
                   jax.ShapeDtypeStruct((m, V_W), BF16),
                   jax.ShapeDtypeStruct((m * KV_GROUPS, QK_DIM), F32),
                   jax.ShapeDtypeStruct((m, V_W), F32)],
        scratch_shapes=[pltpu.VMEM((tm, d), BF16)],
        compiler_params=_params("parallel", "arbitrary"),
        name="qkv_proj",
    )(x, g.reshape(1, d), w, w, w)


def _proj_res_kernel(a_ref, w_ref, r_ref, o_ref):
    o_ref[...] = r_ref[...] + jnp.dot(a_ref[...].astype(BF16), w_ref[...],
                                      preferred_element_type=F32)


def _proj_res(a, w, layer, res):
    m, k = a.shape
    n = w.shape[2]
    tm = min(m, RES_ROW_TILE)
    return pl.pallas_call(
        _proj_res_kernel,
        grid=(m // tm,),
        in_specs=[pl.BlockSpec((tm, k), lambda i: (i, 0)),
                  pl.BlockSpec((None, k, n), lambda i: (layer, 0, 0)),
                  pl.BlockSpec((tm, n), lambda i: (i, 0))],
        out_specs=pl.BlockSpec((tm, n), lambda i: (i, 0)),
        out_shape=jax.ShapeDtypeStruct((m, n), F32),
        compiler_params=_params("parallel"),
        name="proj_res",
    )(a, w, res)


def _mlp_step(xn_ref, wu_ref, wd_ref, o_ref):
    h = jnp.dot(xn_ref[...], wu_ref[...], preferred_element_type=F32)
    h = jnp.square(jnp.maximum(h, 0.0)).astype(BF16)
    o_ref[...] += jnp.dot(h, wd_ref[...], preferred_element_type=F32)


def _mlp_kernel(x_ref, g_ref, wu_ref, wd_ref, gf_ref, o_ref, xn_ref, *, final_norm):
    @pl.when(pl.program_id(1) == 0)
    def _():
        x = x_ref[...]
        xn_ref[...] = _rms(x, g_ref[...]).astype(BF16)
        o_ref[...] = x

    _mlp_step(xn_ref, wu_ref, wd_ref, o_ref)

    if final_norm:
        @pl.when(pl.program_id(1) == pl.num_programs(1) - 1)
        def _():
            o_ref[...] = _rms(o_ref[...], gf_ref[...])


def _mlp(x, g, w_up, w_down, layer, g_final, final_norm):
    m, d = x.shape
    ff = w_up.shape[2]
    tm = min(m, ROW_TILE)
    tf = FF_TILE
    return pl.pallas_call(
        functools.partial(_mlp_kernel, final_norm=final_norm),
        grid=(m // tm, ff // tf),
        in_specs=[pl.BlockSpec((tm, d), lambda i, f: (i, 0)),
                  pl.BlockSpec((1, d), lambda i, f: (0, 0)),
                  pl.BlockSpec((None, d, tf), lambda i, f: (layer, 0, f)),
                  pl.BlockSpec((None, tf, d), lambda i, f: (layer, f, 0)),
                  pl.BlockSpec((1, d), lambda i, f: (0, 0))],
        out_specs=pl.BlockSpec((tm, d), lambda i, f: (i, 0)),
        out_shape=jax.ShapeDtypeStruct((m, d), F32),
        scratch_shapes=[pltpu.VMEM((tm, d), BF16)],
        compiler_params=_params("parallel", "arbitrary"),
        name="mlp",
    )(x, g.reshape(1, d), w_up, w_down, g_final.reshape(1, d))


def _sigmoid(x):
    return 1.0 / (1.0 + jnp.exp(-x))


def _gelu_tanh(x):
    c = math.sqrt(2.0 / math.pi)
    return x * (0.5 * (1.0 + jnp.tanh(c * (x + 0.044715 * (x * x * x)))))


def _rg_kernel(xb_ref, gb_ref, cbuf_ref, h0_ref, cw_ref, cb_ref, wa_ref, ba_ref, wi_ref, bi_ref,
               lam_ref, y_ref, cout_ref, hout_ref, win_ref, *, seq, tt):
    win_ref[0:CONV_PAD, :] = jnp.zeros((CONV_PAD, RG_BLOCK), F32)
    win_ref[CONV_PAD - (CONV_W - 1):CONV_PAD, :] = cbuf_ref[...]
    win_ref[CONV_PAD:CONV_PAD + seq, :] = xb_ref[...]
    cout_ref[...] = win_ref[CONV_PAD + seq - (CONV_W - 1):CONV_PAD + seq, :]

    cw = cw_ref[...]
    cb = cb_ref[...]
    wa = wa_ref[...]
    wi = wi_ref[...]
    ba = ba_ref[...]
    bi = bi_ref[...]
    neg_lam = -lam_ref[...]
    log_a_scale = -LRU_C * (jnp.maximum(neg_lam, 0.0) + jnp.log(1.0 + jnp.exp(-jnp.abs(neg_lam))))

    def chunk(base, h):
        w = win_ref[pl.ds(base, tt + CONV_PAD), :]
        u = cb
        for k in range(CONV_W):
            off = CONV_PAD - (CONV_W - 1) + k
            u = u + w[off:off + tt] * cw[k:k + 1]
        ub = u.astype(BF16)
        r = _sigmoid(jnp.dot(ub, wa, preferred_element_type=F32) + ba)
        i = _sigmoid(jnp.dot(ub, wi, preferred_element_type=F32) + bi)
        a = jnp.exp(log_a_scale * r)
        b = jnp.sqrt(1.0 - a * a) * (i * u)
        if tt <= SUBLANES:
            rows = []
            for t in range(tt):
                h = a[t:t + 1] * h + b[t:t + 1]
                rows.append(h)
            hseq = jnp.concatenate(rows, axis=0)
        else:
            sub = lax.broadcasted_iota(jnp.int32, (SUBLANES, RG_BLOCK), 0)
            rows = []
            for g in range(tt // SUBLANES):
                ag = a[g * SUBLANES:(g + 1) * SUBLANES]
                bg = b[g * SUBLANES:(g + 1) * SUBLANES]
                d = 1
                while d < SUBLANES:
                    keep = sub >= d
                    a_sh = jnp.where(keep, pltpu.roll(ag, d, 0), 1.0)
                    b_sh = jnp.where(keep, pltpu.roll(bg, d, 0), 0.0)
                    bg = ag * b_sh + bg
                    ag = ag * a_sh
                    d *= 2
                hg = ag * h + bg
                h = hg[SUBLANES - 1:SUBLANES]
                rows.append(hg)
            hseq = jnp.concatenate(rows, axis=0)
        gate = _gelu_tanh(gb_ref[pl.ds(base, tt), :])
        y_ref[pl.ds(base, tt), :] = (hseq * gate).astype(y_ref.dtype)
        return h

    h0 = h0_ref[...]
    if seq == tt:
        h_last = chunk(0, h0)
    else:
        h_last = lax.fori_loop(
            0, seq // tt, lambda c, h: chunk(pl.multiple_of(c * tt, tt), h), h0)
    hout_ref[...] = h_last


def _rg_mix(xz, conv_buf, h0, conv_w, conv_b, w_a, b_a, w_i, b_i, lam, y_dtype):
    bsz, seq, _ = xz.shape
    tt = min(seq, SCAN_TILE)
    nb = RG_BLOCKS
    vec = lambda: pl.BlockSpec((1, RG_BLOCK), lambda b, j: (0, j))
    win_rows = -(-(seq + CONV_PAD) // SUBLANES) * SUBLANES
    return pl.pallas_call(
        functools.partial(_rg_kernel, seq=seq, tt=tt),
        grid=(bsz, nb),
        in_specs=[pl.BlockSpec((None, seq, RG_BLOCK), lambda b, j: (b, 0, j)),
                  pl.BlockSpec((None, seq, RG_BLOCK), lambda b, j: (b, 0, nb + j)),
                  pl.BlockSpec((None, CONV_W - 1, RG_BLOCK), lambda b, j: (b, 0, j)),
                  pl.BlockSpec((None, 1, RG_BLOCK), lambda b, j: (b, 0, j)),
                  pl.BlockSpec((CONV_W, RG_BLOCK), lambda b, j: (0, j)),
                  vec(),
                  pl.BlockSpec((None, RG_BLOCK, RG_BLOCK), lambda b, j: (j, 0, 0)),
                  vec(),
                  pl.BlockSpec((None, RG_BLOCK, RG_BLOCK), lambda b, j: (j, 0, 0)),
                  vec(),
                  vec()],
        out_specs=[pl.BlockSpec((None, seq, RG_BLOCK), lambda b, j: (b, 0, j)),
                   pl.BlockSpec((None, CONV_W - 1, RG_BLOCK), lambda b, j: (b, 0, j)),
                   pl.BlockSpec((None, 1, RG_BLOCK), lambda b, j: (b, 0, j))],
        out_shape=[jax.ShapeDtypeStruct((bsz, seq, D_RNN), y_dtype),
                   jax.ShapeDtypeStruct((bsz, CONV_W - 1, D_RNN), F32),
                   jax.ShapeDtypeStruct((bsz, 1, D_RNN), F32)],
        scratch_shapes=[pltpu.VMEM((win_rows, RG_BLOCK), F32)],
        compiler_params=_params("parallel", "parallel"),
        name="rg_mix",
    )(xz, xz, conv_buf, h0.reshape(bsz, 1, D_RNN), conv_w, conv_b.reshape(1, D_RNN),
      w_a, b_a.reshape(1, D_RNN), w_i, b_i.reshape(1, D_RNN), lam.reshape(1, D_RNN))


def _diff_lambda(lq1_ref, lk1_ref, lq2_ref, lk2_ref, lam_init):
    s1 = jnp.sum(lq1_ref[...] * lk1_ref[...], axis=-1, keepdims=True)
    s2 = jnp.sum(lq2_ref[...] * lk2_ref[...], axis=-1, keepdims=True)
    return jnp.exp(s1) - jnp.exp(s2) + lam_init


def _head_norm(o, subln, lam_init):
    on = o * lax.rsqrt(jnp.mean(o * o, axis=-1, keepdims=True) + EPS)
    return on * subln * (1.0 - lam_init)


_NT = (((1,), (1,)), ((), ()))


def _lane_tiles(x, n):
    return x if n == 1 else jnp.concatenate([x] * n, axis=1)


def _softmax_update(s, v_blocks, m_ref, l_ref, acc_ref):
    n = s.shape[1] // LANES
    m_prev = m_ref[...]
    m_new = jnp.maximum(m_prev, jnp.max(s, axis=-1, keepdims=True))
    alpha = jnp.exp2(m_prev - m_new)
    p = jnp.exp2(s - _lane_tiles(m_new, n))
    p_lanes = p[:, :LANES]
    for c in range(1, n):
        p_lanes = p_lanes + p[:, c * LANES:(c + 1) * LANES]
    l_ref[...] = alpha * l_ref[...] + p_lanes
    pb = p.astype(BF16)
    pv = None
    for off, cols, vb in v_blocks:
        part = jnp.dot(pb[:, off:off + cols], vb, preferred_element_type=F32)
        pv = part if pv is None else pv + part
    acc_ref[...] = _lane_tiles(alpha, acc_ref.shape[1] // LANES) * acc_ref[...] + pv
    m_ref[...] = m_new


def _prompt_attn_kernel(q_ref, kb_ref, vb_ref, lq1_ref, lk1_ref, lq2_ref, lk2_ref, subln_ref,
                        o_ref, m_ref, l_ref, acc_ref, *, tq, lam_init):
    qi = pl.program_id(2)

    m_ref[...] = jnp.full(m_ref.shape, -jnp.inf, F32)
    l_ref[...] = jnp.zeros(l_ref.shape, F32)
    acc_ref[...] = jnp.zeros(acc_ref.shape, F32)

    q = q_ref[...]
    q1, q2 = q[:, :QK_DIM], q[:, QK_DIM:]

    def block(j, masked):
        start = pl.multiple_of(j * tq, tq)
        kj = kb_ref[pl.ds(start, tq), :]
        vj = vb_ref[pl.ds(start, tq), :]
        s1 = lax.dot_general(q1, kj[:, :QK_DIM], _NT, preferred_element_type=F32)
        s2 = lax.dot_general(q2, kj[:, QK_DIM:], _NT, preferred_element_type=F32)
        s = jnp.concatenate([s1, s2], axis=0)
        if masked:
            row = lax.broadcasted_iota(jnp.int32, (tq, tq), 0)
            col = lax.broadcasted_iota(jnp.int32, (tq, tq), 1)
            keep = col <= row
            s = jnp.where(jnp.concatenate([keep, keep], axis=0), s, -jnp.inf)
        _softmax_update(s, [(0, tq, vj)], m_ref, l_ref, acc_ref)

    def body(j, carry):
        block(j, False)
        return carry

    lax.fori_loop(0, qi, body, 0)
    block(qi, True)

    lam = _diff_lambda(lq1_ref, lk1_ref, lq2_ref, lk2_ref, lam_init)
    pn = acc_ref[...] / jnp.sum(l_ref[...], axis=-1, keepdims=True)
    o = pn[:tq] - lam * pn[tq:]
    o_ref[...] = _head_norm(o, subln_ref[...], lam_init).astype(o_ref.dtype)


def _prompt_attn(q, k, v, lq1, lk1, lq2, lk2, subln, lam_init, bsz, seq):
    m = bsz * seq
    tq = Q_TILE
    nq = seq // tq
    lvec = lambda: pl.BlockSpec((1, QK_DIM), lambda b, h, i: (0, 0))
    return pl.pallas_call(
        functools.partial(_prompt_attn_kernel, tq=tq, lam_init=lam_init),
        grid=(bsz, N_HEADS, nq),
        in_specs=[pl.BlockSpec((tq, V_DIM), lambda b, h, i: (b * nq + i, h)),
                  pl.BlockSpec((seq, V_DIM), lambda b, h, i: (b, h)),
                  pl.BlockSpec((seq, V_DIM), lambda b, h, i: (b, h)),
                  lvec(), lvec(), lvec(), lvec(),
                  pl.BlockSpec((1, V_DIM), lambda b, h, i: (0, 0))],
        out_specs=pl.BlockSpec((tq, V_DIM), lambda b, h, i: (b * nq + i, h)),
        out_shape=jax.ShapeDtypeStruct((m, V_W), BF16),
        scratch_shapes=[pltpu.VMEM((2 * tq, LANES), F32),
                        pltpu.VMEM((2 * tq, LANES), F32),
                        pltpu.VMEM((2 * tq, V_DIM), F32)],
        compiler_params=_params("parallel", "parallel", "arbitrary"),
        name="prompt_attn",
    )(q, k, v, lq1.reshape(1, -1), lk1.reshape(1, -1), lq2.reshape(1, -1), lk2.reshape(1, -1),
      subln.reshape(1, -1))


def _decode_scores(qb_ref, bias, k_page, n_cols):
    per_map = []
    for mp in range(2):
        km = k_page[pl.ds(mp, n_cols, stride=2), :].astype(BF16)
        per_map.append(lax.dot_general(qb_ref[mp], km, _NT, preferred_element_type=F32))
    return jnp.concatenate(per_map, axis=0) + bias


def _mlp_attn_kernel(pt_ref, x_hbm, g_ref, wu_ref, wd_ref, gf_ref,
                     q_ref, m_in, l_in, acc_in, knew_ref, vnew_ref,
                     lq1_ref, lk1_ref, lq2_ref, lk2_ref, subln_ref, *refs,
                     n_pages, dec_seq, lam_init, tm, final_norm, first_half):
    del pt_ref
    k_refs, v_refs = refs[:n_pages], refs[n_pages:2 * n_pages]
    rest = refs[2 * n_pages:]
    if first_half:
        o_ref, m_out, l_out, acc_out = rest[:4]
        rest = rest[4:]
    else:
        o_ref, heads_ref = rest[:2]
        rest = rest[2:]
    xn_ref, x_sem, qb_ref, bias_ref, m_ref, l_ref, acc_ref = rest
    i = pl.program_id(0)
    f = pl.program_id(1)
    n_q = N_HEADS * dec_seq

    @pl.when(f == 0)
    def _():
        cp = pltpu.make_async_copy(x_hbm.at[pl.ds(pl.multiple_of(i * tm, tm), tm), :], o_ref, x_sem)
        cp.start()
        qb_ref[...] = (q_ref[...] * (QK_SCALE * LOG2_E)).astype(BF16)
        row = lax.broadcasted_iota(jnp.int32, bias_ref.shape, 0)
        col = lax.broadcasted_iota(jnp.int32, bias_ref.shape, 1)
        same_head = (row % n_q) // dec_seq == col % N_HEADS
        bias_ref[...] = jnp.where(same_head, 0.0, -jnp.inf)
        if first_half:
            m_ref[...] = jnp.full(m_ref.shape, -jnp.inf, F32)
            l_ref[...] = jnp.zeros(l_ref.shape, F32)
            acc_ref[...] = jnp.zeros(acc_ref.shape, F32)
        else:
            m_ref[...] = m_in[...]
            l_ref[...] = l_in[...]
            acc_ref[...] = acc_in[...]
        cp.wait()
        xn_ref[...] = _rms(o_ref[...], g_ref[...]).astype(BF16)

    bias = bias_ref[...]
    s = jnp.concatenate([_decode_scores(qb_ref, bias, kp, V_ROWS) for kp in k_refs], axis=1)
    v_blocks = [(p * V_ROWS, V_ROWS, vp[...].astype(BF16)) for p, vp in enumerate(v_refs)]
    _softmax_update(s, v_blocks, m_ref, l_ref, acc_ref)

    _mlp_step(xn_ref, wu_ref, wd_ref, o_ref)

    @pl.when(f == pl.num_programs(1) - 1)
    def _():
        if final_norm:
            o_ref[...] = _rms(o_ref[...], gf_ref[...])
        if first_half:
            m_out[...] = m_ref[...]
            l_out[...] = l_ref[...]
            acc_out[...] = acc_ref[...]
        else:
            s_new = _decode_scores(qb_ref, bias_ref[:, :LANES], knew_ref, LANES)
            t = lax.broadcasted_iota(jnp.int32, s_new.shape, 0) % dec_seq
            tok = lax.broadcasted_iota(jnp.int32, s_new.shape, 1) // N_HEADS
            s_new = jnp.where(tok <= t, s_new, -jnp.inf)
            _softmax_update(s_new, [(0, LANES, vnew_ref[...].astype(BF16))], m_ref, l_ref, acc_ref)
            lam = _diff_lambda(lq1_ref, lk1_ref, lq2_ref, lk2_ref, lam_init)
            pn = acc_ref[...] / jnp.sum(l_ref[...], axis=-1, keepdims=True)
            o = pn[:n_q] - lam * pn[n_q:]
            heads_ref[...] = _head_norm(o, subln_ref[...], lam_init)


def _mlp_attn(x, g, w_up, w_down, layer, g_final, final_norm,
              q_arr, state, k_page, v_page, cache_k, cache_v, cache_layer, page_table,
              lq1, lk1, lq2, lk2, subln, lam_init, dec_seq):
    m, d = x.shape
    ff = w_up.shape[2]
    tm, tf, gp = ROW_TILE, FF_TILE, PAGES_PER_STEP
    bsz = q_arr.shape[0]
    n_q = N_HEADS * dec_seq
    n_steps = ff // tf
    first_half = state is None
    assert m // tm == bsz and 2 * n_steps * gp == page_table.shape[1]
    page0 = 0 if first_half else n_steps * gp

    def page_spec(p, rows, width):
        return pl.BlockSpec((None, None, rows, width),
                            lambda i, f, pt, p=p: (cache_layer, pt[i, page0 + f * gp + p], 0, 0))

    const = lambda shape: pl.BlockSpec(shape, lambda i, f, pt: (0,) * len(shape))
    per_seq = lambda rows, width: pl.BlockSpec((None, rows, width), lambda i, f, pt: (i, 0, 0))
    state_specs = [per_seq(2 * n_q, LANES), per_seq(2 * n_q, LANES), per_seq(2 * n_q, V_DIM)]
    state_shapes = [jax.ShapeDtypeStruct((bsz, 2 * n_q, LANES), F32),
                    jax.ShapeDtypeStruct((bsz, 2 * n_q, LANES), F32),
                    jax.ShapeDtypeStruct((bsz, 2 * n_q, V_DIM), F32)]
    if first_half:
        state = [jnp.zeros(s.shape, F32) for s in state_shapes]
        extra_specs, extra_shapes = state_specs, state_shapes
    else:
        extra_specs = [per_seq(n_q, V_DIM)]
        extra_shapes = [jax.ShapeDtypeStruct((bsz, n_q, V_DIM), F32)]
    grid_spec = pltpu.PrefetchScalarGridSpec(
        num_scalar_prefetch=1,
        grid=(m // tm, n_steps),
        in_specs=[pl.BlockSpec(memory_space=pl.ANY),
                  const((1, d)),
                  pl.BlockSpec((None, d, tf), lambda i, f, pt: (layer, 0, f)),
                  pl.BlockSpec((None, tf, d), lambda i, f, pt: (layer, f, 0)),
                  const((1, d)),
                  pl.BlockSpec((None, 2, n_q, QK_DIM), lambda i, f, pt: (i, 0, 0, 0))]
                 + state_specs
                 + [per_seq(2 * LANES, QK_DIM), per_seq(LANES, V_DIM),
                    const((1, QK_DIM)), const((1, QK_DIM)), const((1, QK_DIM)), const((1, QK_DIM)),
                    const((1, V_DIM))]
                 + [page_spec(p, K_ROWS, QK_DIM) for p in range(gp)]
                 + [page_spec(p, V_ROWS, V_DIM) for p in range(gp)],
        out_specs=[pl.BlockSpec((tm, d), lambda i, f, pt: (i, 0))] + extra_specs,
        scratch_shapes=[pltpu.VMEM((tm, d), BF16),
                        pltpu.SemaphoreType.DMA(()),
                        pltpu.VMEM((2, n_q, QK_DIM), BF16),
                        pltpu.VMEM((2 * n_q, V_ROWS), F32),
                        pltpu.VMEM((2 * n_q, LANES), F32),
                        pltpu.VMEM((2 * n_q, LANES), F32),
                        pltpu.VMEM((2 * n_q, V_DIM), F32)],
    )
    return pl.pallas_call(
        functools.partial(_mlp_attn_kernel, n_pages=gp, dec_seq=dec_seq, lam_init=lam_init, tm=tm,
                          final_norm=final_norm, first_half=first_half),
        grid_spec=grid_spec,
        out_shape=[jax.ShapeDtypeStruct((m, d), F32)] + extra_shapes,
        compiler_params=_params("parallel", "arbitrary"),
        name="mlp_attn",
    )(page_table, x, g.reshape(1, d), w_up, w_down, g_final.reshape(1, d), q_arr, *state,
      k_page, v_page, lq1.reshape(1, -1), lk1.reshape(1, -1), lq2.reshape(1, -1),
      lk2.reshape(1, -1), subln.reshape(1, -1), *([cache_k] * gp), *([cache_v] * gp))


def _decode_operands(q, k_new, v_new):
    bsz, dec_seq, _ = q.shape
    q_arr = q.reshape(bsz, dec_seq, N_HEADS, 2, QK_DIM).transpose(0, 3, 2, 1, 4)
    q_arr = q_arr.reshape(bsz, 2, N_HEADS * dec_seq, QK_DIM)
    pad_tokens = NEW_TOKENS_PAD - dec_seq
    k_page = jnp.pad(k_new.reshape(bsz, dec_seq * N_HEADS * 2, QK_DIM),
                     ((0, 0), (0, pad_tokens * N_HEADS * 2), (0, 0)))
    v_page = jnp.pad(v_new.reshape(bsz, dec_seq * N_HEADS, V_DIM),
                     ((0, 0), (0, pad_tokens * N_HEADS), (0, 0)))
    return q_arr, k_page, v_page


def kernel(x_prompt, x_sample, cache_k, cache_v, state_conv, state_h, page_table, norm_mix, norm_mlp, norm_final, rg_w_in, rg_conv_w, rg_conv_b, rg_w_a, rg_b_a, rg_w_i, rg_b_i, rg_lambda, rg_w_out, da_w_qkv, da_lq1, da_lk1, da_lq2, da_lk2, da_subln, da_w_o, mlp_w_up, mlp_w_down):
    bsz, seq, d = x_prompt.shape
    dbsz, dseq, _ = x_sample.shape
    xp = x_prompt.reshape(bsz * seq, d)
    xs = x_sample.reshape(dbsz * dseq, d)
    n_layers_b, n_pool = cache_k.shape[:2]
    ck = cache_k.reshape(n_layers_b, n_pool, K_ROWS, QK_DIM)
    cv = cache_v.reshape(n_layers_b, n_pool, V_ROWS, V_DIM)

    w_in, w_out = rg_w_in.astype(BF16), rg_w_out.astype(BF16)
    w_a, w_i = rg_w_a.astype(BF16), rg_w_i.astype(BF16)
    w_qkv, w_o = da_w_qkv.astype(BF16), da_w_o.astype(BF16)
    w_up, w_down = mlp_w_up.astype(BF16), mlp_w_down.astype(BF16)

    kp_l, vp_l, cp_l, hp_l = [], [], [], []
    ks_l, vs_l, cs_l, hs_l = [], [], [], []

    def rg_params(j):
        return (rg_conv_w[j], rg_conv_b[j], w_a[j], rg_b_a[j], w_i[j], rg_b_i[j], rg_lambda[j])

    def sample_rg_layer(xs, i):
        j = i // 2
        (xz,) = _norm_proj(xs, norm_mix[i], w_in, j, 2 * D_RNN, [F32], [1.0])
        ys, cbs, hls = _rg_mix(xz.reshape(dbsz, dseq, 2 * D_RNN), state_conv[j], state_h[j],
                               *rg_params(j), F32)
        cs_l.append(cbs)
        hs_l.append(hls.reshape(dbsz, D_RNN))
        xs = _proj_res(ys.reshape(dbsz * dseq, D_RNN), w_out, j, xs)
        return _mlp(xs, norm_mlp[i], w_up, w_down, i, norm_final, False)

    def sample_qkv(xs, i):
        j = i // 2
        qs, kn, vn = _norm_proj(xs, norm_mix[i], w_qkv, j, QK_W, [F32, F32, F32], [1.0, 1.0, 1.0])
        ks_l.append(kn.reshape(dbsz, dseq, N_HEADS, 2, QK_DIM))
        vs_l.append(vn.reshape(dbsz, dseq, N_HEADS, V_DIM))
        return _decode_operands(qs.reshape(dbsz, dseq, QK_W), kn.reshape(dbsz, dseq, QK_W),
                                vn.reshape(dbsz, dseq, V_W))

    def sample_attn_out(xs, heads, i, last):
        j = i // 2
        os_ = heads.reshape(dbsz, N_HEADS, dseq, V_DIM).transpose(0, 2, 1, 3).reshape(dbsz * dseq, V_W)
        xs = _proj_res(os_, w_o, j, xs)
        return _mlp(xs, norm_mlp[i], w_up, w_down, i, norm_final, last)

    for i in range(DEPTH):
        j = i // 2
        if i % 2 == 0:
            xs = sample_rg_layer(xs, i)
            dec = sample_qkv(xs, i + 1)
            dj = (i + 1) // 2
            dec_lam_init = _lambda_init(i + 1)
            dec_vecs = (da_lq1[dj], da_lk1[dj], da_lq2[dj], da_lk2[dj], da_subln[dj])

            (xz_p,) = _norm_proj(xp, norm_mix[i], w_in, j, 2 * D_RNN, [F32], [1.0])
            yp, cbp, hlp = _rg_mix(xz_p.reshape(bsz, seq, 2 * D_RNN),
                                   jnp.zeros((bsz, CONV_W - 1, D_RNN), F32),
                                   jnp.zeros((bsz, D_RNN), F32), *rg_params(j), BF16)
            cp_l.append(cbp)
            hp_l.append(hlp.reshape(bsz, D_RNN))
            xp = _proj_res(yp.reshape(bsz * seq, D_RNN), w_out, j, xp)
            xp, *dec_state = _mlp_attn(xp, norm_mlp[i], w_up, w_down, i, norm_final, False,
                                       dec[0], None, dec[1], dec[2], ck, cv, dj, page_table,
                                       *dec_vecs, dec_lam_init, dseq)
        else:
            lam_init = _lambda_init(i)
            lvecs = (da_lq1[j], da_lk1[j], da_lq2[j], da_lk2[j], da_subln[j])
            qp, kb, vb, kp, vp = _qkv_proj(xp, norm_mix[i], w_qkv, j, QK_SCALE * LOG2_E)
            op = _prompt_attn(qp, kb, vb, *lvecs, lam_init, bsz, seq)
            kp_l.append(kp.reshape(bsz, seq, N_HEADS, 2, QK_DIM))
            vp_l.append(vp.reshape(bsz, seq, N_HEADS, V_DIM))
            xp = _proj_res(op, w_o, j, xp)
            last = i == DEPTH - 1
            xp, heads = _mlp_attn(xp, norm_mlp[i], w_up, w_down, i, norm_final, last,
                                  dec[0], dec_state, dec[1], dec[2], ck, cv, j, page_table,
                                  *lvecs, lam_init, dseq)
            xs = sample_attn_out(xs, heads, i, last)
    return (xp.reshape(bsz, seq, d), xs.reshape(dbsz, dseq, d),
            jnp.stack(kp_l), jnp.stack(vp_l), jnp.stack(cp_l), jnp.stack(hp_l),
            jnp.stack(ks_l), jnp.stack(vs_l), jnp.stack(cs_l), jnp.stack(hs_l))
```

```python
import functools
import math

import jax
import jax.numpy as jnp
from jax import lax
from jax.experimental import pallas as pl
from jax.experimental.pallas import tpu as pltpu

F32 = jnp.float32
BF16 = jnp.bfloat16

D_MODEL = 2048
DEPTH = 4
PAGE_SIZE = 128
D_RNN = D_MODEL
RG_BLOCK = 256
RG_BLOCKS = D_RNN // RG_BLOCK
CONV_W = 4
LRU_C = 8.0
QK_DIM = 128
V_DIM = 2 * QK_DIM
N_HEADS = D_MODEL // V_DIM
QK_W = N_HEADS * 2 * QK_DIM
V_W = N_HEADS * V_DIM
D_FF = 4 * D_MODEL
EPS = 1e-6
QK_SCALE = QK_DIM ** -0.5

VMEM_LIMIT_BYTES = 56 * 1024 * 1024
SUBLANES = 8
LANES = 128
LOG2_E = math.log2(math.e)

ROW_TILE = 1024
RES_ROW_TILE = 512
COL_TILE = 1024
GROUP_COL_TILE = 512
FF_TILE = 512
Q_TILE = 512
SCAN_TILE = 256
PAGES_PER_STEP = 4
CONV_PAD = SUBLANES

KV_GROUPS = N_HEADS * 2
K_ROWS = PAGE_SIZE * KV_GROUPS
V_ROWS = PAGE_SIZE * N_HEADS
NEW_TOKENS_PAD = LANES // N_HEADS


def _params(*semantics):
    return pltpu.CompilerParams(dimension_semantics=semantics,
                                vmem_limit_bytes=VMEM_LIMIT_BYTES)


def _rms(x, g):
    return x * lax.rsqrt(jnp.mean(x * x, axis=-1, keepdims=True) + EPS) * g


def _lambda_init(layer_idx):
    return 0.8 - 0.6 * math.exp(-0.3 * layer_idx)


def _norm_proj_kernel(x_ref, g_ref, *refs, n_out, scales):
    w_refs, o_refs, xn_ref = refs[:n_out], refs[n_out:2 * n_out], refs[2 * n_out]

    @pl.when(pl.program_id(1) == 0)
    def _():
        xn_ref[...] = _rms(x_ref[...], g_ref[...]).astype(BF16)

    xn = xn_ref[...]
    for w_ref, o_ref, scale in zip(w_refs, o_refs, scales):
        acc = jnp.dot(xn, w_ref[...], preferred_element_type=F32)
        if scale != 1.0:
            acc = acc * scale
        o_ref[...] = acc.astype(o_ref.dtype)


def _norm_proj(x, g, w, layer, group_width, out_dtypes, scales):
    m, d = x.shape
    n_out = len(out_dtypes)
    assert w.shape[1:] == (d, n_out * group_width)
    tm = min(m, ROW_TILE)
    tn = COL_TILE if n_out == 1 else GROUP_COL_TILE
    nj = group_width // tn
    in_specs = [pl.BlockSpec((tm, d), lambda i, j: (i, 0)),
                pl.BlockSpec((1, d), lambda i, j: (0, 0))]
    for k in range(n_out):
        in_specs.append(pl.BlockSpec((None, d, tn), lambda i, j, k=k: (layer, 0, k * nj + j)))
    return pl.pallas_call(
        functools.partial(_norm_proj_kernel, n_out=n_out, scales=tuple(scales)),
        grid=(m // tm, nj),
        in_specs=in_specs,
        out_specs=[pl.BlockSpec((tm, tn), lambda i, j: (i, j))] * n_out,
        out_shape=[jax.ShapeDtypeStruct((m, group_width), dt) for dt in out_dtypes],
        scratch_shapes=[pltpu.VMEM((tm, d), BF16)],
        compiler_params=_params("parallel", "arbitrary"),
        name="norm_proj",
    )(x, g.reshape(1, d), *([w] * n_out))


def _qkv_proj_kernel(x_ref, g_ref, wq_ref, wk_ref, wv_ref, kprev_ref, vprev_ref,
                     q_ref, kb_ref, vb_ref, kf_ref, v_ref, xn_ref, *, q_scale):
    del kprev_ref, vprev_ref
    j = pl.program_id(1)

    @pl.when(j == 0)
    def _():
        xn_ref[...] = _rms(x_ref[...], g_ref[...]).astype(BF16)

    xn = xn_ref[...]
    q = jnp.dot(xn, wq_ref[...], preferred_element_type=F32)
    q_ref[...] = (q * q_scale).astype(BF16)
    k = jnp.dot(xn, wk_ref[...], preferred_element_type=F32)
    kb_ref[...] = k.astype(BF16)
    v = jnp.dot(xn, wv_ref[...], preferred_element_type=F32)
    vb_ref[...] = v.astype(BF16)
    v_ref[...] = v
    tm, tn = k.shape
    for c in range(tn // QK_DIM):
        kf_ref[pl.ds(j * (tn // QK_DIM) + c, tm, stride=KV_GROUPS), :] = k[:, c * QK_DIM:(c + 1) * QK_DIM]


def _qkv_proj(x, g, w, layer, q_scale, stacked):
    m, d = x.shape
    tm, tn = RES_ROW_TILE, GROUP_COL_TILE
    nj = QK_W // tn
    ni = m // tm
    w_spec = lambda k: pl.BlockSpec((None, d, tn), lambda i, j, k=k: (layer, 0, k * nj + j))
    tile = pl.BlockSpec((tm, tn), lambda i, j: (i, j))
    return pl.pallas_call(
        functools.partial(_qkv_proj_kernel, q_scale=q_scale),
        grid=(ni, nj),
        in_specs=[pl.BlockSpec((tm, d), lambda i, j: (i, 0)),
                  pl.BlockSpec((1, d), lambda i, j: (0, 0)),
                  w_spec(0), w_spec(1), w_spec(2),
                  pl.BlockSpec(memory_space=pl.ANY), pl.BlockSpec(memory_space=pl.ANY)],
        out_specs=[tile, tile, tile,
                   pl.BlockSpec((tm * KV_GROUPS, QK_DIM), lambda i, j: (layer * ni + i, 0)),
                   pl.BlockSpec((tm, tn), lambda i, j: (layer * ni + i, j))],
        out_shape=[jax.ShapeDtypeStruct((m, QK_W), BF16),
                   jax.ShapeDtypeStruct((m, QK_W), BF16),
                   jax.ShapeDtypeStruct((m, V_W), BF16),
                   jax.ShapeDtypeStruct(stacked[0].shape, F32),
                   jax.ShapeDtypeStruct(stacked[1].shape, F32)],
        input_output_aliases={5: 3, 6: 4},
        scratch_shapes=[pltpu.VMEM((tm, d), BF16)],
        compiler_params=_params("parallel", "arbitrary"),
        name="qkv_proj",
    )(x, g.reshape(1, d), w, w, w, *stacked)


def _proj_res_kernel(a_ref, w_ref, r_ref, o_ref):
    o_ref[...] = r_ref[...] + jnp.dot(a_ref[...].astype(BF16), w_ref[...],
                                      preferred_element_type=F32)


def _proj_res(a, w, layer, res):
    m, k = a.shape
    n = w.shape[2]
    tm = min(m, RES_ROW_TILE)
    return pl.pallas_call(
        _proj_res_kernel,
        grid=(m // tm,),
        in_specs=[pl.BlockSpec((tm, k), lambda i: (i, 0)),
                  pl.BlockSpec((None, k, n), lambda i: (layer, 0, 0)),
                  pl.BlockSpec((tm, n), lambda i: (i, 0))],
        out_specs=pl.BlockSpec((tm, n), lambda i: (i, 0)),
        out_shape=jax.ShapeDtypeStruct((m, n), F32),
        compiler_params=_params("parallel"),
        name="proj_res",
    )(a, w, res)


def _mlp_step(xn_ref, wu_ref, wd_ref, o_ref):
    h = jnp.dot(xn_ref[...], wu_ref[...], preferred_element_type=F32)
    h = jnp.square(jnp.maximum(h, 0.0)).astype(BF16)
    o_ref[...] += jnp.dot(h, wd_ref[...], preferred_element_type=F32)


def _mlp_kernel(x_ref, g_ref, wu_ref, wd_ref, gf_ref, o_ref, xn_ref, *, final_norm):
    @pl.when(pl.program_id(1) == 0)
    def _():
        x = x_ref[...]
        xn_ref[...] = _rms(x, g_ref[...]).astype(BF16)
        o_ref[...] = x

    _mlp_step(xn_ref, wu_ref, wd_ref, o_ref)

    if final_norm:
        @pl.when(pl.program_id(1) == pl.num_programs(1) - 1)
        def _():
            o_ref[...] = _rms(o_ref[...], gf_ref[...])


def _mlp(x, g, w_up, w_down, layer, g_final, final_norm):
    m, d = x.shape
    ff = w_up.shape[2]
    tm = min(m, ROW_TILE)
    tf = FF_TILE
    return pl.pallas_call(
        functools.partial(_mlp_kernel, final_norm=final_norm),
        grid=(m // tm, ff // tf),
        in_specs=[pl.BlockSpec((tm, d), lambda i, f: (i, 0)),
                  pl.BlockSpec((1, d), lambda i, f: (0, 0)),
                  pl.BlockSpec((None, d, tf), lambda i, f: (layer, 0, f)),
                  pl.BlockSpec((None, tf, d), lambda i, f: (layer, f, 0)),
                  pl.BlockSpec((1, d), lambda i, f: (0, 0))],
        out_specs=pl.BlockSpec((tm, d), lambda i, f: (i, 0)),
        out_shape=jax.ShapeDtypeStruct((m, d), F32),
        scratch_shapes=[pltpu.VMEM((tm, d), BF16)],
        compiler_params=_params("parallel", "arbitrary"),
        name="mlp",
    )(x, g.reshape(1, d), w_up, w_down, g_final.reshape(1, d))


def _sigmoid(x):
    return 1.0 / (1.0 + jnp.exp(-x))


def _gelu_tanh(x):
    c = math.sqrt(2.0 / math.pi)
    return x * (0.5 * (1.0 + jnp.tanh(c * (x + 0.044715 * (x * x * x)))))


def _rg_kernel(xb_ref, gb_ref, cbuf_ref, h0_ref, cw_ref, cb_ref, wa_ref, ba_ref, wi_ref, bi_ref,
               lam_ref, y_ref, cout_ref, hout_ref, win_ref, *, seq, tt):
    win_ref[0:CONV_PAD, :] = jnp.zeros((CONV_PAD, RG_BLOCK), F32)
    win_ref[CONV_PAD - (CONV_W - 1):CONV_PAD, :] = cbuf_ref[...]
    win_ref[CONV_PAD:CONV_PAD + seq, :] = xb_ref[...]
    cout_ref[...] = win_ref[CONV_PAD + seq - (CONV_W - 1):CONV_PAD + seq, :]

    cw = cw_ref[...]
    cb = cb_ref[...]
    wa = wa_ref[...]
    wi = wi_ref[...]
    ba = ba_ref[...]
    bi = bi_ref[...]
    neg_lam = -lam_ref[...]
    log_a_scale = -LRU_C * (jnp.maximum(neg_lam, 0.0) + jnp.log(1.0 + jnp.exp(-jnp.abs(neg_lam))))

    def chunk(base, h):
        w = win_ref[pl.ds(base, tt + CONV_PAD), :]
        u = cb
        for k in range(CONV_W):
            off = CONV_PAD - (CONV_W - 1) + k
            u = u + w[off:off + tt] * cw[k:k + 1]
        ub = u.astype(BF16)
        r = _sigmoid(jnp.dot(ub, wa, preferred_element_type=F32) + ba)
        i = _sigmoid(jnp.dot(ub, wi, preferred_element_type=F32) + bi)
        a = jnp.exp(log_a_scale * r)
        b = jnp.sqrt(1.0 - a * a) * (i * u)
        if tt <= SUBLANES:
            rows = []
            for t in range(tt):
                h = a[t:t + 1] * h + b[t:t + 1]
                rows.append(h)
            hseq = jnp.concatenate(rows, axis=0)
        else:
            sub = lax.broadcasted_iota(jnp.int32, (SUBLANES, RG_BLOCK), 0)
            rows = []
            for g in range(tt // SUBLANES):
                ag = a[g * SUBLANES:(g + 1) * SUBLANES]
                bg = b[g * SUBLANES:(g + 1) * SUBLANES]
                d = 1
                while d < SUBLANES:
                    keep = sub >= d
                    a_sh = jnp.where(keep, pltpu.roll(ag, d, 0), 1.0)
                    b_sh = jnp.where(keep, pltpu.roll(bg, d, 0), 0.0)
                    bg = ag * b_sh + bg
                    ag = ag * a_sh
                    d *= 2
                hg = ag * h + bg
                h = hg[SUBLANES - 1:SUBLANES]
                rows.append(hg)
            hseq = jnp.concatenate(rows, axis=0)
        gate = _gelu_tanh(gb_ref[pl.ds(base, tt), :])
        y_ref[pl.ds(base, tt), :] = (hseq * gate).astype(y_ref.dtype)
        return h

    h0 = h0_ref[...]
    if seq == tt:
        h_last = chunk(0, h0)
    else:
        h_last = lax.fori_loop(
            0, seq // tt, lambda c, h: chunk(pl.multiple_of(c * tt, tt), h), h0)
    hout_ref[...] = h_last


def _rg_mix(xz, conv_buf, h0, conv_w, conv_b, w_a, b_a, w_i, b_i, lam, y_dtype):
    bsz, seq, _ = xz.shape
    tt = min(seq, SCAN_TILE)
    nb = RG_BLOCKS
    vec = lambda: pl.BlockSpec((1, RG_BLOCK), lambda b, j: (0, j))
    win_rows = -(-(seq + CONV_PAD) // SUBLANES) * SUBLANES
    return pl.pallas_call(
        functools.partial(_rg_kernel, seq=seq, tt=tt),
        grid=(bsz, nb),
        in_specs=[pl.BlockSpec((None, seq, RG_BLOCK), lambda b, j: (b, 0, j)),
                  pl.BlockSpec((None, seq, RG_BLOCK), lambda b, j: (b, 0, nb + j)),
                  pl.BlockSpec((None, CONV_W - 1, RG_BLOCK), lambda b, j: (b, 0, j)),
                  pl.BlockSpec((None, 1, RG_BLOCK), lambda b, j: (b, 0, j)),
                  pl.BlockSpec((CONV_W, RG_BLOCK), lambda b, j: (0, j)),
                  vec(),
                  pl.BlockSpec((None, RG_BLOCK, RG_BLOCK), lambda b, j: (j, 0, 0)),
                  vec(),
                  pl.BlockSpec((None, RG_BLOCK, RG_BLOCK), lambda b, j: (j, 0, 0)),
                  vec(),
                  vec()],
        out_specs=[pl.BlockSpec((None, seq, RG_BLOCK), lambda b, j: (b, 0, j)),
                   pl.BlockSpec((None, CONV_W - 1, RG_BLOCK), lambda b, j: (b, 0, j)),
                   pl.BlockSpec((None, 1, RG_BLOCK), lambda b, j: (b, 0, j))],
        out_shape=[jax.ShapeDtypeStruct((bsz, seq, D_RNN), y_dtype),
                   jax.ShapeDtypeStruct((bsz, CONV_W - 1, D_RNN), F32),
                   jax.ShapeDtypeStruct((bsz, 1, D_RNN), F32)],
        scratch_shapes=[pltpu.VMEM((win_rows, RG_BLOCK), F32)],
        compiler_params=_params("parallel", "parallel"),
        name="rg_mix",
    )(xz, xz, conv_buf, h0.reshape(bsz, 1, D_RNN), conv_w, conv_b.reshape(1, D_RNN),
      w_a, b_a.reshape(1, D_RNN), w_i, b_i.reshape(1, D_RNN), lam.reshape(1, D_RNN))


def _diff_lambda(lq1_ref, lk1_ref, lq2_ref, lk2_ref, lam_init):
    s1 = jnp.sum(lq1_ref[...] * lk1_ref[...], axis=-1, keepdims=True)
    s2 = jnp.sum(lq2_ref[...] * lk2_ref[...], axis=-1, keepdims=True)
    return jnp.exp(s1) - jnp.exp(s2) + lam_init


def _head_norm(o, subln, lam_init):
    on = o * lax.rsqrt(jnp.mean(o * o, axis=-1, keepdims=True) + EPS)
    return on * subln * (1.0 - lam_init)


_NT = (((1,), (1,)), ((), ()))


def _lane_tiles(x, n):
    return x if n == 1 else jnp.concatenate([x] * n, axis=1)


def _softmax_update(s, v_blocks, m_ref, l_ref, acc_ref):
    n = s.shape[1] // LANES
    m_prev = m_ref[...]
    m_new = jnp.maximum(m_prev, jnp.max(s, axis=-1, keepdims=True))
    alpha = jnp.exp2(m_prev - m_new)
    p = jnp.exp2(s - _lane_tiles(m_new, n))
    p_lanes = p[:, :LANES]
    for c in range(1, n):
        p_lanes = p_lanes + p[:, c * LANES:(c + 1) * LANES]
    l_ref[...] = alpha * l_ref[...] + p_lanes
    pb = p.astype(BF16)
    pv = None
    for off, cols, vb in v_blocks:
        part = jnp.dot(pb[:, off:off + cols], vb, preferred_element_type=F32)
        pv = part if pv is None else pv + part
    acc_ref[...] = _lane_tiles(alpha, acc_ref.shape[1] // LANES) * acc_ref[...] + pv
    m_ref[...] = m_new


def _prompt_attn_kernel(q_ref, kb_ref, vb_ref, lq1_ref, lk1_ref, lq2_ref, lk2_ref, subln_ref,
                        o_ref, m_ref, l_ref, acc_ref, *, tq, lam_init):
    qi = pl.program_id(2)

    m_ref[...] = jnp.full(m_ref.shape, -jnp.inf, F32)
    l_ref[...] = jnp.zeros(l_ref.shape, F32)
    acc_ref[...] = jnp.zeros(acc_ref.shape, F32)

    q = q_ref[...]
    q1, q2 = q[:, :QK_DIM], q[:, QK_DIM:]

    def block(j, masked):
        start = pl.multiple_of(j * tq, tq)
        kj = kb_ref[pl.ds(start, tq), :]
        vj = vb_ref[pl.ds(start, tq), :]
        s1 = lax.dot_general(q1, kj[:, :QK_DIM], _NT, preferred_element_type=F32)
        s2 = lax.dot_general(q2, kj[:, QK_DIM:], _NT, preferred_element_type=F32)
        s = jnp.concatenate([s1, s2], axis=0)
        if masked:
            row = lax.broadcasted_iota(jnp.int32, (tq, tq), 0)
            col = lax.broadcasted_iota(jnp.int32, (tq, tq), 1)
            keep = col <= row
            s = jnp.where(jnp.concatenate([keep, keep], axis=0), s, -jnp.inf)
        _softmax_update(s, [(0, tq, vj)], m_ref, l_ref, acc_ref)

    def body(j, carry):
        block(j, False)
        return carry

    lax.fori_loop(0, qi, body, 0)
    block(qi, True)

    lam = _diff_lambda(lq1_ref, lk1_ref, lq2_ref, lk2_ref, lam_init)
    pn = acc_ref[...] / jnp.sum(l_ref[...], axis=-1, keepdims=True)
    o = pn[:tq] - lam * pn[tq:]
    o_ref[...] = _head_norm(o, subln_ref[...], lam_init).astype(o_ref.dtype)


def _prompt_attn(q, k, v, lq1, lk1, lq2, lk2, subln, lam_init, bsz, seq):
    m = bsz * seq
    tq = Q_TILE
    nq = seq // tq
    lvec = lambda: pl.BlockSpec((1, QK_DIM), lambda b, h, i: (0, 0))
    return pl.pallas_call(
        functools.partial(_prompt_attn_kernel, tq=tq, lam_init=lam_init),
        grid=(bsz, N_HEADS, nq),
        in_specs=[pl.BlockSpec((tq, V_DIM), lambda b, h, i: (b * nq + i, h)),
                  pl.BlockSpec((seq, V_DIM), lambda b, h, i: (b, h)),
                  pl.BlockSpec((seq, V_DIM), lambda b, h, i: (b, h)),
                  lvec(), lvec(), lvec(), lvec(),
                  pl.BlockSpec((1, V_DIM), lambda b, h, i: (0, 0))],
        out_specs=pl.BlockSpec((tq, V_DIM), lambda b, h, i: (b * nq + i, h)),
        out_shape=jax.ShapeDtypeStruct((m, V_W), BF16),
        scratch_shapes=[pltpu.VMEM((2 * tq, LANES), F32),
                        pltpu.VMEM((2 * tq, LANES), F32),
                        pltpu.VMEM((2 * tq, V_DIM), F32)],
        compiler_params=_params("parallel", "parallel", "arbitrary"),
        name="prompt_attn",
    )(q, k, v, lq1.reshape(1, -1), lk1.reshape(1, -1), lq2.reshape(1, -1), lk2.reshape(1, -1),
      subln.reshape(1, -1))


def _decode_scores(qb_ref, bias, k_page, n_cols):
    per_map = []
    for mp in range(2):
        km = k_page[pl.ds(mp, n_cols, stride=2), :].astype(BF16)
        per_map.append(lax.dot_general(qb_ref[mp], km, _NT, preferred_element_type=F32))
    return jnp.concatenate(per_map, axis=0) + bias


def _mlp_attn_kernel(pt_ref, x_hbm, g_ref, wu_ref, wd_ref, gf_ref,
                     q_ref, m_in, l_in, acc_in, knew_ref, vnew_ref,
                     lq1_ref, lk1_ref, lq2_ref, lk2_ref, subln_ref, *refs,
                     n_pages, dec_seq, lam_init, tm, final_norm, first_half):
    del pt_ref
    k_refs, v_refs = refs[:n_pages], refs[n_pages:2 * n_pages]
    rest = refs[2 * n_pages:]
    if first_half:
        o_ref, m_out, l_out, acc_out = rest[:4]
        rest = rest[4:]
    else:
        o_ref, heads_ref = rest[:2]
        rest = rest[2:]
    xn_ref, x_sem, qb_ref, bias_ref, m_ref, l_ref, acc_ref = rest
    i = pl.program_id(0)
    f = pl.program_id(1)
    n_q = N_HEADS * dec_seq

    @pl.when(f == 0)
    def _():
        cp = pltpu.make_async_copy(x_hbm.at[pl.ds(pl.multiple_of(i * tm, tm), tm), :], o_ref, x_sem)
        cp.start()
        qb_ref[...] = (q_ref[...] * (QK_SCALE * LOG2_E)).astype(BF16)
        row = lax.broadcasted_iota(jnp.int32, bias_ref.shape, 0)
        col = lax.broadcasted_iota(jnp.int32, bias_ref.shape, 1)
        same_head = (row % n_q) // dec_seq == col % N_HEADS
        bias_ref[...] = jnp.where(same_head, 0.0, -jnp.inf)
        if first_half:
            m_ref[...] = jnp.full(m_ref.shape, -jnp.inf, F32)
            l_ref[...] = jnp.zeros(l_ref.shape, F32)
            acc_ref[...] = jnp.zeros(acc_ref.shape, F32)
        else:
            m_ref[...] = m_in[...]
            l_ref[...] = l_in[...]
            acc_ref[...] = acc_in[...]
        cp.wait()
        xn_ref[...] = _rms(o_ref[...], g_ref[...]).astype(BF16)

    bias = bias_ref[...]
    s = jnp.concatenate([_decode_scores(qb_ref, bias, kp, V_ROWS) for kp in k_refs], axis=1)
    v_blocks = [(p * V_ROWS, V_ROWS, vp[...].astype(BF16)) for p, vp in enumerate(v_refs)]
    _softmax_update(s, v_blocks, m_ref, l_ref, acc_ref)

    _mlp_step(xn_ref, wu_ref, wd_ref, o_ref)

    @pl.when(f == pl.num_programs(1) - 1)
    def _():
        if final_norm:
            o_ref[...] = _rms(o_ref[...], gf_ref[...])
        if first_half:
            m_out[...] = m_ref[...]
            l_out[...] = l_ref[...]
            acc_out[...] = acc_ref[...]
        else:
            s_new = _decode_scores(qb_ref, bias_ref[:, :LANES], knew_ref, LANES)
            t = lax.broadcasted_iota(jnp.int32, s_new.shape, 0) % dec_seq
            tok = lax.broadcasted_iota(jnp.int32, s_new.shape, 1) // N_HEADS
            s_new = jnp.where(tok <= t, s_new, -jnp.inf)
            _softmax_update(s_new, [(0, LANES, vnew_ref[...].astype(BF16))], m_ref, l_ref, acc_ref)
            lam = _diff_lambda(lq1_ref, lk1_ref, lq2_ref, lk2_ref, lam_init)
            pn = acc_ref[...] / jnp.sum(l_ref[...], axis=-1, keepdims=True)
            o = pn[:n_q] - lam * pn[n_q:]
            heads_ref[...] = _head_norm(o, subln_ref[...], lam_init)


def _mlp_attn(x, g, w_up, w_down, layer, g_final, final_norm,
              q_arr, state, k_page, v_page, cache_k, cache_v, cache_layer, page_table,
              lq1, lk1, lq2, lk2, subln, lam_init, dec_seq):
    m, d = x.shape
    ff = w_up.shape[2]
    tm, tf, gp = ROW_TILE, FF_TILE, PAGES_PER_STEP
    bsz = q_arr.shape[0]
    n_q = N_HEADS * dec_seq
    n_steps = ff // tf
    first_half = state is None
    assert m // tm == bsz and 2 * n_steps * gp == page_table.shape[1]
    page0 = 0 if first_half else n_steps * gp

    def page_spec(p, rows, width):
        return pl.BlockSpec((None, None, rows, width),
                            lambda i, f, pt, p=p: (cache_layer, pt[i, page0 + f * gp + p], 0, 0))

    const = lambda shape: pl.BlockSpec(shape, lambda i, f, pt: (0,) * len(shape))
    per_seq = lambda rows, width: pl.BlockSpec((None, rows, width), lambda i, f, pt: (i, 0, 0))
    state_specs = [per_seq(2 * n_q, LANES), per_seq(2 * n_q, LANES), per_seq(2 * n_q, V_DIM)]
    state_shapes = [jax.ShapeDtypeStruct((bsz, 2 * n_q, LANES), F32),
                    jax.ShapeDtypeStruct((bsz, 2 * n_q, LANES), F32),
                    jax.ShapeDtypeStruct((bsz, 2 * n_q, V_DIM), F32)]
    if first_half:
        state = [jnp.zeros(s.shape, F32) for s in state_shapes]
        extra_specs, extra_shapes = state_specs, state_shapes
    else:
        extra_specs = [per_seq(n_q, V_DIM)]
        extra_shapes = [jax.ShapeDtypeStruct((bsz, n_q, V_DIM), F32)]
    grid_spec = pltpu.PrefetchScalarGridSpec(
        num_scalar_prefetch=1,
        grid=(m // tm, n_steps),
        in_specs=[pl.BlockSpec(memory_space=pl.ANY),
                  const((1, d)),
                  pl.BlockSpec((None, d, tf), lambda i, f, pt: (layer, 0, f)),
                  pl.BlockSpec((None, tf, d), lambda i, f, pt: (layer, f, 0)),
                  const((1, d)),
                  pl.BlockSpec((None, 2, n_q, QK_DIM), lambda i, f, pt: (i, 0, 0, 0))]
                 + state_specs
                 + [per_seq(2 * LANES, QK_DIM), per_seq(LANES, V_DIM),
                    const((1, QK_DIM)), const((1, QK_DIM)), const((1, QK_DIM)), const((1, QK_DIM)),
                    const((1, V_DIM))]
                 + [page_spec(p, K_ROWS, QK_DIM) for p in range(gp)]
                 + [page_spec(p, V_ROWS, V_DIM) for p in range(gp)],
        out_specs=[pl.BlockSpec((tm, d), lambda i, f, pt: (i, 0))] + extra_specs,
        scratch_shapes=[pltpu.VMEM((tm, d), BF16),
                        pltpu.SemaphoreType.DMA(()),
                        pltpu.VMEM((2, n_q, QK_DIM), BF16),
                        pltpu.VMEM((2 * n_q, V_ROWS), F32),
                        pltpu.VMEM((2 * n_q, LANES), F32),
                        pltpu.VMEM((2 * n_q, LANES), F32),
                        pltpu.VMEM((2 * n_q, V_DIM), F32)],
    )
    return pl.pallas_call(
        functools.partial(_mlp_attn_kernel, n_pages=gp, dec_seq=dec_seq, lam_init=lam_init, tm=tm,
                          final_norm=final_norm, first_half=first_half),
        grid_spec=grid_spec,
        out_shape=[jax.ShapeDtypeStruct((m, d), F32)] + extra_shapes,
        compiler_params=_params("parallel", "arbitrary"),
        name="mlp_attn",
    )(page_table, x, g.reshape(1, d), w_up, w_down, g_final.reshape(1, d), q_arr, *state,
      k_page, v_page, lq1.reshape(1, -1), lk1.reshape(1, -1), lq2.reshape(1, -1),
      lk2.reshape(1, -1), subln.reshape(1, -1), *([cache_k] * gp), *([cache_v] * gp))


def _decode_operands(q, k_new, v_new):
    bsz, dec_seq, _ = q.shape
    q_arr = q.reshape(bsz, dec_seq, N_HEADS, 2, QK_DIM).transpose(0, 3, 2, 1, 4)
    q_arr = q_arr.reshape(bsz, 2, N_HEADS * dec_seq, QK_DIM)
    pad_tokens = NEW_TOKENS_PAD - dec_seq
    k_page = jnp.pad(k_new.reshape(bsz, dec_seq * N_HEADS * 2, QK_DIM),
                     ((0, 0), (0, pad_tokens * N_HEADS * 2), (0, 0)))
    v_page = jnp.pad(v_new.reshape(bsz, dec_seq * N_HEADS, V_DIM),
                     ((0, 0), (0, pad_tokens * N_HEADS), (0, 0)))
    return q_arr, k_page, v_page


def kernel(x_prompt, x_sample, cache_k, cache_v, state_conv, state_h, page_table, norm_mix, norm_mlp, norm_final, rg_w_in, rg_conv_w, rg_conv_b, rg_w_a, rg_b_a, rg_w_i, rg_b_i, rg_lambda, rg_w_out, da_w_qkv, da_lq1, da_lk1, da_lq2, da_lk2, da_subln, da_w_o, mlp_w_up, mlp_w_down):
    bsz, seq, d = x_prompt.shape
    dbsz, dseq, _ = x_sample.shape
    xp = x_prompt.reshape(bsz * seq, d)
    xs = x_sample.reshape(dbsz * dseq, d)
    n_layers_b, n_pool = cache_k.shape[:2]
    ck = cache_k.reshape(n_layers_b, n_pool, K_ROWS, QK_DIM)
    cv = cache_v.reshape(n_layers_b, n_pool, V_ROWS, V_DIM)

    w_in, w_out = rg_w_in.astype(BF16), rg_w_out.astype(BF16)
    w_a, w_i = rg_w_a.astype(BF16), rg_w_i.astype(BF16)
    w_qkv, w_o = da_w_qkv.astype(BF16), da_w_o.astype(BF16)
    w_up, w_down = mlp_w_up.astype(BF16), mlp_w_down.astype(BF16)

    kv_stacked = [jnp.zeros((n_layers_b * bsz * seq * KV_GROUPS, QK_DIM), F32),
                  jnp.zeros((n_layers_b * bsz * seq, V_W), F32)]
    cp_l, hp_l = [], []
    ks_l, vs_l, cs_l, hs_l = [], [], [], []

    def rg_params(j):
        return (rg_conv_w[j], rg_conv_b[j], w_a[j], rg_b_a[j], w_i[j], rg_b_i[j], rg_lambda[j])

    def sample_rg_layer(xs, i):
        j = i // 2
        (xz,) = _norm_proj(xs, norm_mix[i], w_in, j, 2 * D_RNN, [F32], [1.0])
        ys, cbs, hls = _rg_mix(xz.reshape(dbsz, dseq, 2 * D_RNN), state_conv[j], state_h[j],
                               *rg_params(j), F32)
        cs_l.append(cbs)
        hs_l.append(hls.reshape(dbsz, D_RNN))
        xs = _proj_res(ys.reshape(dbsz * dseq, D_RNN), w_out, j, xs)
        return _mlp(xs, norm_mlp[i], w_up, w_down, i, norm_final, False)

    def sample_qkv(xs, i):
        j = i // 2
        qs, kn, vn = _norm_proj(xs, norm_mix[i], w_qkv, j, QK_W, [F32, F32, F32], [1.0, 1.0, 1.0])
        ks_l.append(kn.reshape(dbsz, dseq, N_HEADS, 2, QK_DIM))
        vs_l.append(vn.reshape(dbsz, dseq, N_HEADS, V_DIM))
        return _decode_operands(qs.reshape(dbsz, dseq, QK_W), kn.reshape(dbsz, dseq, QK_W),
                                vn.reshape(dbsz, dseq, V_W))

    def sample_attn_out(xs, heads, i, last):
        j = i // 2
        os_ = heads.reshape(dbsz, N_HEADS, dseq, V_DIM).transpose(0, 2, 1, 3).reshape(dbsz * dseq, V_W)
        xs = _proj_res(os_, w_o, j, xs)
        return _mlp(xs, norm_mlp[i], w_up, w_down, i, norm_final, last)

    for i in range(DEPTH):
        j = i // 2
        if i % 2 == 0:
            xs = sample_rg_layer(xs, i)
            dec = sample_qkv(xs, i + 1)
            dj = (i + 1) // 2
            dec_lam_init = _lambda_init(i + 1)
            dec_vecs = (da_lq1[dj], da_lk1[dj], da_lq2[dj], da_lk2[dj], da_subln[dj])

            (xz_p,) = _norm_proj(xp, norm_mix[i], w_in, j, 2 * D_RNN, [F32], [1.0])
            yp, cbp, hlp = _rg_mix(xz_p.reshape(bsz, seq, 2 * D_RNN),
                                   jnp.zeros((bsz, CONV_W - 1, D_RNN), F32),
                                   jnp.zeros((bsz, D_RNN), F32), *rg_params(j), BF16)
            cp_l.append(cbp)
            hp_l.append(hlp.reshape(bsz, D_RNN))
            xp = _proj_res(yp.reshape(bsz * seq, D_RNN), w_out, j, xp)
            xp, *dec_state = _mlp_attn(xp, norm_mlp[i], w_up, w_down, i, norm_final, False,
                                       dec[0], None, dec[1], dec[2], ck, cv, dj, page_table,
                                       *dec_vecs, dec_lam_init, dseq)
        else:
            lam_init = _lambda_init(i)
            lvecs = (da_lq1[j], da_lk1[j], da_lq2[j], da_lk2[j], da_subln[j])
            qp, kb, vb, *kv_stacked = _qkv_proj(xp, norm_mix[i], w_qkv, j, QK_SCALE * LOG2_E, kv_stacked)
            op = _prompt_attn(qp, kb, vb, *lvecs, lam_init, bsz, seq)
            xp = _proj_res(op, w_o, j, xp)
            last = i == DEPTH - 1
            xp, heads = _mlp_attn(xp, norm_mlp[i], w_up, w_down, i, norm_final, last,
                                  dec[0], dec_state, dec[1], dec[2], ck, cv, j, page_table,
                                  *lvecs, lam_init, dseq)
            xs = sample_attn_out(xs, heads, i, last)
    k_prompt = kv_stacked[0].reshape(n_layers_b, bsz, seq, N_HEADS, 2, QK_DIM)
    v_prompt = kv_stacked[1].reshape(n_layers_b, bsz, seq, N_HEADS, V_DIM)
    return (xp.reshape(bsz, seq, d), xs.reshape(dbsz, dseq, d),
            k_prompt, v_prompt, jnp.stack(cp_l), jnp.stack(hp_l),
            jnp.stack(ks_l), jnp.stack(vs_l), jnp.stack(cs_l), jnp.stack(hs_l))
```
